```python
import jax, jax.numpy as jnp
from jax import lax
import numpy as np

D_MODEL = 1024
BATCH = 8
SEQ = 4096
DEPTH = 1

N_META = 16
CHUNK = 128
META_PAD = CHUNK - N_META
SSM_EXPAND = 2
D_INNER = SSM_EXPAND * D_MODEL
SSM_HEAD_DIM = 64
SSM_HEADS = D_INNER // SSM_HEAD_DIM
SSM_GROUPS = 4
D_STATE = 128
SSM_CONV = 4
CONV_DIM = D_INNER + 2 * SSM_GROUPS * D_STATE
ATTN_HEADS = 16
ATTN_KV_HEADS = 4
ATTN_HEAD_DIM = 64
ATTN_GROUP = ATTN_HEADS // ATTN_KV_HEADS
WINDOW = 128
ATTN_WIDTH = ATTN_HEADS * ATTN_HEAD_DIM
FFN_DIM = 2816
FFN_CONV = 3
N_IN = D_INNER + CONV_DIM + SSM_HEADS + (ATTN_HEADS + 2 * ATTN_KV_HEADS) * ATTN_HEAD_DIM + 2 * D_MODEL
EPS = 1e-6
NEG = -1e30

kernel_name = "hybrid_ssd_swa_sink_alibi_convffn"


def _rmsnorm(x, w):
    xf = x.astype(jnp.float32)
    y = xf * lax.rsqrt(jnp.mean(xf * xf, axis=-1, keepdims=True) + EPS)
    return (y * w.astype(jnp.float32)).astype(x.dtype)


def _causal_dwconv(u, w, b):
    k_width = w.shape[0]
    seq_len = u.shape[1]
    up = jnp.pad(u, ((0, 0), (k_width - 1, 0), (0, 0)))
    out = b.astype(u.dtype) + w[k_width - 1].astype(u.dtype) * u
    for k in range(k_width - 1):
        out = out + w[k].astype(u.dtype) * up[:, k:k + seq_len]
    return out


def _segsum_exp(a):
    t = a.shape[-1]
    cs = jnp.cumsum(a, axis=-1)
    mask = jnp.tril(jnp.ones((t, t), dtype=bool))
    diff = cs[..., :, None] - cs[..., None, :]
    return jnp.where(mask, jnp.exp(jnp.where(mask, diff, 0.0)), 0.0)


def _ssd_chunked(x_dt, a_dt, b_in, c_in):
    bsz, lp, n_heads, p_dim = x_dt.shape
    g, n = b_in.shape[2], b_in.shape[3]
    r = n_heads // g
    nc = lp // CHUNK
    xc = x_dt.reshape(bsz, nc, CHUNK, g, r, p_dim)
    ac = a_dt.reshape(bsz, nc, CHUNK, g, r).transpose(0, 3, 4, 1, 2)
    bc = b_in.reshape(bsz, nc, CHUNK, g, n)
    cc = c_in.reshape(bsz, nc, CHUNK, g, n)
    a_cs = jnp.cumsum(ac, axis=-1)
    lmat = _segsum_exp(ac)
    cb = jnp.einsum("bclgn,bcsgn->bgcls", cc, bc)
    y_diag = jnp.einsum("bgcls,bgrcls,bcsgrp->bclgrp", cb, lmat, xc)
    decay_states = jnp.exp(a_cs[..., -1:] - a_cs)
    states = jnp.einsum("bclgn,bgrcl,bclgrp->bcgrpn", bc, decay_states, xc)
    chunk_decay = jnp.exp(a_cs[..., -1])

    def step(h, inp):
        s_c, d_c = inp
        return h * d_c[..., None, None] + s_c, h

    h0 = jnp.zeros_like(states[:, 0])
    _, h_in = lax.scan(step, h0, (jnp.moveaxis(states, 1, 0), jnp.moveaxis(chunk_decay, -1, 0)))
    h_in = jnp.moveaxis(h_in, 0, 1)
    y_off = jnp.einsum("bclgn,bcgrpn,bgrcl->bclgrp", cc, h_in, jnp.exp(a_cs))
    return (y_diag + y_off).reshape(bsz, lp, n_heads, p_dim)


def _ssd_branch(z, xbc, dt_raw, conv_w, conv_b, dt_bias, a_log, d_skip, norm_w, w_out):
    bsz, seq_len, _ = xbc.shape
    xbc = jax.nn.silu(_causal_dwconv(xbc, conv_w, conv_b))
    xs, bs, cs = jnp.split(xbc, [D_INNER, D_INNER + SSM_GROUPS * D_STATE], axis=-1)
    xs = xs.reshape(bsz, seq_len, SSM_HEADS, SSM_HEAD_DIM).astype(jnp.float32)
    bs = bs.reshape(bsz, seq_len, SSM_GROUPS, D_STATE).astype(jnp.float32)
    cs = cs.reshape(bsz, seq_len, SSM_GROUPS, D_STATE).astype(jnp.float32)
    dt = jax.nn.softplus(dt_raw.astype(jnp.float32) + dt_bias.astype(jnp.float32))
    a = -jnp.exp(a_log.astype(jnp.float32))
    pad4 = ((0, 0), (META_PAD, 0), (0, 0), (0, 0))
    x_dt = jnp.pad(xs * dt[..., None], pad4)
    a_dt = jnp.pad(dt * a, ((0, 0), (META_PAD, 0), (0, 0)))
    y = _ssd_chunked(x_dt, a_dt, jnp.pad(bs, pad4), jnp.pad(cs, pad4))[:, META_PAD:]
    y = y + xs * d_skip.astype(jnp.float32)[:, None]
    y = y.reshape(bsz, seq_len, D_INNER).astype(z.dtype)
    y = _rmsnorm(y * jax.nn.silu(z), norm_w)
    return y @ w_out


def _swa_branch(q, k, v, sinks, w_out):
    bsz, seq_len, _ = q.shape
    lp = seq_len + META_PAD
    nb = lp // CHUNK
    scale = ATTN_HEAD_DIM ** -0.5
    q = q.reshape(bsz, seq_len, ATTN_KV_HEADS, ATTN_GROUP, ATTN_HEAD_DIM)
    k = k.reshape(bsz, seq_len, ATTN_KV_HEADS, ATTN_HEAD_DIM)
    v = v.reshape(bsz, seq_len, ATTN_KV_HEADS, ATTN_HEAD_DIM)
    qb = jnp.pad(q, ((0, 0), (META_PAD, 0), (0, 0), (0, 0), (0, 0))).reshape(
        bsz, nb, CHUNK, ATTN_KV_HEADS, ATTN_GROUP, ATTN_HEAD_DIM)
    kpad = ((0, 0), (META_PAD + CHUNK, 0), (0, 0), (0, 0))
    kp = jnp.pad(k, kpad).reshape(bsz, nb + 1, CHUNK, ATTN_KV_HEADS, ATTN_HEAD_DIM)
    vp = jnp.pad(v, kpad).reshape(bsz, nb + 1, CHUNK, ATTN_KV_HEADS, ATTN_HEAD_DIM)
    kb = jnp.concatenate([kp[:, :-1], kp[:, 1:]], axis=2)
    vb = jnp.concatenate([vp[:, :-1], vp[:, 1:]], axis=2)
    q_pos = jnp.arange(nb)[:, None] * CHUNK + jnp.arange(CHUNK)[None, :] - META_PAD
    k_pos = jnp.arange(nb)[:, None] * CHUNK + jnp.arange(2 * CHUNK)[None, :] - CHUNK - META_PAD
    dist = q_pos[:, :, None] - k_pos[:, None, :]
    band_ok = (dist >= 0) & (dist < WINDOW) & (k_pos[:, None, :] >= N_META)
    slopes = jnp.exp2(-8.0 * jnp.arange(1, ATTN_HEADS + 1, dtype=jnp.float32) / ATTN_HEADS)
    slopes = slopes.reshape(ATTN_KV_HEADS, ATTN_GROUP)
    s_band = jnp.einsum("bnqkgd,bnskd->bnkgqs", qb, kb, preferred_element_type=jnp.float32) * scale
    s_band = s_band - slopes[None, None, :, :, None, None] * dist.astype(jnp.float32)[None, :, None, None]
    s_band = jnp.where(band_ok[None, :, None, None], s_band, NEG)
    k_meta, v_meta = k[:, :N_META], v[:, :N_META]
    s_meta = jnp.einsum("bnqkgd,bmkd->bnkgqm", qb, k_meta, preferred_element_type=jnp.float32) * scale
    meta_ok = jnp.arange(N_META)[None, None, :] <= q_pos[:, :, None]
    s_meta = jnp.where(meta_ok[None, :, None, None], s_meta, NEG)
    sink = jnp.broadcast_to(sinks.astype(jnp.float32).reshape(ATTN_KV_HEADS, ATTN_GROUP)[None, None, :, :, None, None],
                            s_band.shape[:-1] + (1,))
    probs = jax.nn.softmax(jnp.concatenate([s_meta, s_band, sink], axis=-1), axis=-1).astype(v.dtype)
    out = (jnp.einsum("bnkgqm,bmkd->bnqkgd", probs[..., :N_META], v_meta)
           + jnp.einsum("bnkgqs,bnskd->bnqkgd", probs[..., N_META:N_META + 2 * CHUNK], vb))
    out = out.reshape(bsz, lp, ATTN_WIDTH)[:, META_PAD:]
    return out @ w_out


def _token_mixer(h, w_in, ssm_conv_w, ssm_conv_b, ssm_dt_bias, ssm_a_log, ssm_d_skip, ssm_norm,
                 w_ssm_out, attn_sinks, w_attn_out, w_mix_out):
    sizes = [D_INNER, CONV_DIM, SSM_HEADS, ATTN_WIDTH, ATTN_KV_HEADS * ATTN_HEAD_DIM,
             ATTN_KV_HEADS * ATTN_HEAD_DIM, 2 * D_MODEL]
    cuts = [int(c) for c in np.cumsum(sizes)[:-1]]
    z, xbc, dt_raw, q, k, v, gate_logits = jnp.split(h @ w_in, cuts, axis=-1)
    y_ssm = _ssd_branch(z, xbc, dt_raw, ssm_conv_w, ssm_conv_b, ssm_dt_bias, ssm_a_log, ssm_d_skip,
                        ssm_norm, w_ssm_out)
    y_attn = _swa_branch(q, k, v, attn_sinks, w_attn_out)
    gates = jax.nn.sigmoid(gate_logits.astype(jnp.float32)).astype(h.dtype)
    g_ssm, g_attn = jnp.split(gates, 2, axis=-1)
    return (g_ssm * y_ssm + g_attn * y_attn) @ w_mix_out


def _conv_ffn(h, w_up, conv_w, conv_b, w_down):
    u = _causal_dwconv(h @ w_up, conv_w, conv_b)
    a, g = jnp.split(u, 2, axis=-1)
    return (jax.nn.silu(a) * g) @ w_down


def setup_inputs(seed: int = 0) -> dict:
    key = jax.random.key(seed)
    ks = jax.random.split(key, 24)
    f32 = jnp.float32

    def nrm(k, shape, scale):
        return jax.random.normal(k, shape, f32) * scale

    def gain(k, dim):
        return 1.0 + nrm(k, (DEPTH, dim), 0.01)

    dt0 = jnp.exp(jax.random.uniform(ks[6], (DEPTH, SSM_HEADS), f32, np.log(1e-3), np.log(1e-1)))
    return {
        "x": nrm(ks[0], (BATCH, SEQ, D_MODEL), 1.0),
        "meta_tokens": nrm(ks[1], (N_META, D_MODEL), 1.0),
        "norm_pre_mix": gain(ks[2], D_MODEL),
        "w_in": nrm(ks[3], (DEPTH, D_MODEL, N_IN), D_MODEL ** -0.5),
        "ssm_conv_w": nrm(ks[4], (DEPTH, SSM_CONV, CONV_DIM), 0.5 * SSM_CONV ** -0.5),
        "ssm_conv_b": nrm(ks[5], (DEPTH, CONV_DIM), 0.01),
        "ssm_dt_bias": dt0 + jnp.log(-jnp.expm1(-dt0)),
        "ssm_a_log": jnp.log(jax.random.uniform(ks[7], (DEPTH, SSM_HEADS), f32, 1.0, 16.0)),
        "ssm_d_skip": 1.0 + nrm(ks[8], (DEPTH, SSM_HEADS), 0.01),
        "ssm_norm": gain(ks[9], D_INNER),
        "w_ssm_out": nrm(ks[10], (DEPTH, D_INNER, D_MODEL), D_INNER ** -0.5),
        "attn_sinks": nrm(ks[11], (DEPTH, ATTN_HEADS), 1.0),
        "w_attn_out": nrm(ks[12], (DEPTH, ATTN_WIDTH, D_MODEL), ATTN_WIDTH ** -0.5),
        "w_mix_out": nrm(ks[13], (DEPTH, D_MODEL, D_MODEL), D_MODEL ** -0.5),
        "norm_post_mix": gain(ks[14], D_MODEL),
        "norm_pre_ffn": gain(ks[15], D_MODEL),
        "w_ffn_up": nrm(ks[16], (DEPTH, D_MODEL, 2 * FFN_DIM), D_MODEL ** -0.5),
        "ffn_conv_w": nrm(ks[17], (DEPTH, FFN_CONV, 2 * FFN_DIM), 0.5 * FFN_CONV ** -0.5),
        "ffn_conv_b": nrm(ks[18], (DEPTH, 2 * FFN_DIM), 0.01),
        "w_ffn_down": nrm(ks[19], (DEPTH, FFN_DIM, D_MODEL), FFN_DIM ** -0.5),
        "norm_post_ffn": gain(ks[20], D_MODEL),
    }


def reference(x, meta_tokens, norm_pre_mix, w_in, ssm_conv_w, ssm_conv_b, ssm_dt_bias, ssm_a_log,
              ssm_d_skip, ssm_norm, w_ssm_out, attn_sinks, w_attn_out, w_mix_out, norm_post_mix,
              norm_pre_ffn, w_ffn_up, ffn_conv_w, ffn_conv_b, w_ffn_down, norm_post_ffn):
    bsz = x.shape[0]
    meta = jnp.broadcast_to(meta_tokens.astype(x.dtype)[None], (bsz, N_META, D_MODEL))
    h = jnp.concatenate([meta, x], axis=1)
    for l in range(DEPTH):
        mix = _token_mixer(_rmsnorm(h, norm_pre_mix[l]), w_in[l], ssm_conv_w[l], ssm_conv_b[l],
                           ssm_dt_bias[l], ssm_a_log[l], ssm_d_skip[l], ssm_norm[l], w_ssm_out[l],
                           attn_sinks[l], w_attn_out[l], w_mix_out[l])
        h = h + _rmsnorm(mix, norm_post_mix[l])
        ffn = _conv_ffn(_rmsnorm(h, norm_pre_ffn[l]), w_ffn_up[l], ffn_conv_w[l], ffn_conv_b[l], w_ffn_down[l])
        h = h + _rmsnorm(ffn, norm_post_ffn[l])
    return h[:, N_META:]
```

```python
import functools

import numpy as np
import jax
import jax.numpy as jnp
from jax import lax
from jax.experimental import pallas as pl
from jax.experimental.pallas import tpu as pltpu

D_MODEL = 1024
N_META_TOK = 16
CHUNK = 128
D_INNER = 2048
SSM_HEAD_DIM = 64
SSM_HEADS = D_INNER // SSM_HEAD_DIM
SSM_GROUPS = 4
HEADS_PER_GROUP = SSM_HEADS // SSM_GROUPS
GROUP_WIDTH = HEADS_PER_GROUP * SSM_HEAD_DIM
D_STATE = 128
SSM_CONV = 4
CONV_DIM = D_INNER + 2 * SSM_GROUPS * D_STATE
ATTN_HEADS = 16
ATTN_KV_HEADS = 4
ATTN_HEAD_DIM = 64
ATTN_GROUP = ATTN_HEADS // ATTN_KV_HEADS
ATTN_WIDTH = ATTN_HEADS * ATTN_HEAD_DIM
KV_DUP_WIDTH = ATTN_KV_HEADS * 2 * ATTN_HEAD_DIM
FFN_DIM = 2816
FFN_CONV = 3
FFN_CHUNK = 256
FFN_NCHUNK = FFN_DIM // FFN_CHUNK
EPS = 1e-6
NEG = -1e30

LANES = 128
SUBLANES = 8
TOKEN_TILE = 512
CHUNKS_PER_TILE = TOKEN_TILE // CHUNK
META_CHUNK = CHUNKS_PER_TILE - 1

F32 = jnp.float32
BF16 = jnp.bfloat16


def _dot(a, b):
    return jnp.dot(a, b, preferred_element_type=F32)


def _dot_nt(a, b):
    return lax.dot_general(a, b, (((1,), (1,)), ((), ())), preferred_element_type=F32)


def _dot_tn(a, b):
    return lax.dot_general(a, b, (((0,), (0,)), ((), ())), preferred_element_type=F32)


def _split_bf16(a, terms):
    out = []
    r = a
    for _ in range(terms):
        t = r.astype(BF16)
        out.append(t)
        r = r - t.astype(F32)
    return out


def _rms(x, w):
    return x * lax.rsqrt(jnp.mean(x * x, axis=-1, keepdims=True) + EPS) * w


def _sigmoid(x):
    return 1.0 / (1.0 + jnp.exp(-x))


def _softplus(x):
    return jnp.maximum(x, 0.0) + jnp.log1p(jnp.exp(-jnp.abs(x)))


def _const_spec(shape):
    nd = len(shape)
    return pl.BlockSpec(shape, lambda *_: (0,) * nd, pipeline_mode=pl.Buffered(1))


def _in_proj_kernel(tok_ref, nw_ref, wz_ref, wxbc_ref, wdt_ref, wdtt_ref, wq_ref, wk_ref, wv_ref, wg_ref,
                    z_ref, xbc_ref, dt_ref, dtt_ref, q_ref, k_ref, v_ref, g_ref):
    xn = _rms(tok_ref[...], nw_ref[...]).astype(BF16)
    z_ref[...] = _dot(xn, wz_ref[...]).astype(BF16)
    xbc_ref[...] = _dot(xn, wxbc_ref[...]).astype(BF16)
    dt_ref[...] = _dot(xn, wdt_ref[...])
    dtt_ref[...] = _dot_nt(wdtt_ref[...], xn)
    q_ref[...] = (_dot(xn, wq_ref[...]) * (ATTN_HEAD_DIM ** -0.5)).astype(BF16)
    k_ref[...] = _dot(xn, wk_ref[...]).astype(BF16)
    v_ref[...] = _dot(xn, wv_ref[...]).astype(BF16)
    g_ref[...] = _dot(xn, wg_ref[...]).astype(BF16)


def _in_proj(tok, nw, wz, wxbc, wdt, wdtt, wq, wk, wv, wg):
    rows = tok.shape[0]
    tm = TOKEN_TILE
    row_spec = lambda n: pl.BlockSpec((tm, n), lambda i: (i, 0))
    weights = (nw, wz, wxbc, wdt, wdtt, wq, wk, wv, wg)
    out_widths = (D_INNER, CONV_DIM, None, None, ATTN_WIDTH, KV_DUP_WIDTH, KV_DUP_WIDTH, 2 * D_MODEL)
    out_shape = [
        jax.ShapeDtypeStruct((rows, D_INNER), BF16),
        jax.ShapeDtypeStruct((rows, CONV_DIM), BF16),
        jax.ShapeDtypeStruct((rows, SSM_HEADS), F32),
        jax.ShapeDtypeStruct((SSM_HEADS, rows), F32),
        jax.ShapeDtypeStruct((rows, ATTN_WIDTH), BF16),
        jax.ShapeDtypeStruct((rows, KV_DUP_WIDTH), BF16),
        jax.ShapeDtypeStruct((rows, KV_DUP_WIDTH), BF16),
        jax.ShapeDtypeStruct((rows, 2 * D_MODEL), BF16),
    ]
    out_specs = [
        row_spec(D_INNER), row_spec(CONV_DIM), row_spec(SSM_HEADS),
        pl.BlockSpec((SSM_HEADS, tm), lambda i: (0, i)),
        row_spec(ATTN_WIDTH), row_spec(KV_DUP_WIDTH), row_spec(KV_DUP_WIDTH), row_spec(2 * D_MODEL),
    ]
    del out_widths
    return pl.pallas_call(
        _in_proj_kernel,
        grid=(rows // tm,),
        in_specs=[row_spec(D_MODEL)] + [_const_spec(w.shape) for w in weights],
        out_specs=out_specs,
        out_shape=out_shape,
        compiler_params=pltpu.CompilerParams(
            dimension_semantics=("arbitrary",), vmem_limit_bytes=56 * 1024 * 1024),
        name="in_proj",
    )(tok, *weights)


def _ssd_kernel(xbc_ref, dt_ref, dtt_ref, cw_ref, cb_ref, dtb_ref, dtbt_ref, alog_ref, alogt_ref,
                dskip_ref, expand_ref, y_ref, xbuf, hstate, meta_tail, meta_state, *, chunks_per_batch):
    c = pl.program_id(0)
    t = CHUNK

    @pl.when(c == 0)
    def _():
        xbuf[0:SUBLANES, :] = jnp.zeros((SUBLANES, CONV_DIM), F32)
        hstate[...] = jnp.zeros_like(hstate)

    @pl.when(jnp.logical_and(c > 0, (c - 1) % chunks_per_batch == 0))
    def _():
        xbuf[0:SUBLANES, :] = meta_tail[...]
        hstate[...] = meta_state[...]

    xbuf[SUBLANES:SUBLANES + t, :] = xbc_ref[...].astype(F32)
    cw = cw_ref[...]
    u = cb_ref[...] + cw[SSM_CONV - 1:SSM_CONV, :] * xbuf[SUBLANES:SUBLANES + t, :]
    for k in range(SSM_CONV - 1):
        off = SUBLANES - (SSM_CONV - 1) + k
        u = u + cw[k:k + 1, :] * xbuf[off:off + t, :]
    xbuf[0:SUBLANES, :] = xbuf[t:t + SUBLANES, :]
    u = u * _sigmoid(u)

    row = lax.broadcasted_iota(jnp.int32, (t, 1), 0)
    col = lax.broadcasted_iota(jnp.int32, (1, t), 1)
    live_row = jnp.logical_or(row >= t - N_META_TOK, c > 0)
    live_col = jnp.logical_or(col >= t - N_META_TOK, c > 0)
    u = jnp.where(live_row, u, 0.0)
    xs = u[:, :D_INNER]
    bmat = u[:, D_INNER:D_INNER + SSM_GROUPS * D_STATE].astype(BF16)
    cmat = u[:, D_INNER + SSM_GROUPS * D_STATE:].astype(BF16)

    dt = _softplus(dt_ref[...] + dtb_ref[...])
    adt = jnp.where(live_row, dt * (-jnp.exp(alog_ref[...])), 0.0)
    dtt = _softplus(dtt_ref[...] + dtbt_ref[...])
    adtt = jnp.where(live_col, dtt * (-jnp.exp(alogt_ref[...])), 0.0)

    ri = lax.broadcasted_iota(jnp.int32, (t, t), 0)
    ci = lax.broadcasted_iota(jnp.int32, (t, t), 1)
    causal = ri >= ci
    lower = jnp.where(causal, 1.0, 0.0).astype(BF16)
    upper = jnp.where(ri <= ci, 1.0, 0.0).astype(BF16)
    cs = sum(_dot(lower, p) for p in _split_bf16(adt, 3))
    cst = sum(_dot(p, upper) for p in _split_bf16(adtt, 3))

    ecs = jnp.exp(cs)
    dec = jnp.exp(cs[t - 1:t, :] - cs)
    expand = expand_ref[...]

    def widen(v):
        return sum(_dot(p, expand) for p in _split_bf16(v, 2))

    xdt = xs * widen(dt)
    ecs_w = widen(ecs)
    xdt_b = xdt.astype(BF16)
    xdec_b = (xdt * widen(dec)).astype(BF16)
    lane = lax.broadcasted_iota(jnp.int32, (t, 2 * SSM_HEAD_DIM), 1)
    first_head = lane < SSM_HEAD_DIM

    for g in range(SSM_GROUPS):
        gs = slice(g * GROUP_WIDTH, (g + 1) * GROUP_WIDTH)
        bg = bmat[:, g * D_STATE:(g + 1) * D_STATE]
        cg = cmat[:, g * D_STATE:(g + 1) * D_STATE]
        hg = hstate[:, gs]
        cb = _dot_nt(cg, bg)
        y_off = _dot(cg, hg.astype(BF16)) * ecs_w[:, gs]
        for pr in range(HEADS_PER_GROUP // 2):
            h0 = g * HEADS_PER_GROUP + 2 * pr
            ps = slice(h0 * SSM_HEAD_DIM, (h0 + 2) * SSM_HEAD_DIM)
            decays = []
            for h in (h0, h0 + 1):
                diff = cs[:, h:h + 1] - cst[h:h + 1, :]
                decays.append((cb * jnp.exp(jnp.where(causal, diff, NEG))).astype(BF16))
            lhs = jnp.concatenate(decays, axis=1)
            xp = xdt_b[:, ps]
            zero = jnp.zeros_like(xp)
            rhs = jnp.concatenate([jnp.where(first_head, xp, zero), jnp.where(first_head, zero, xp)], axis=0)
            y_pair = _dot(lhs, rhs) + y_off[:, 2 * pr * SSM_HEAD_DIM:(2 * pr + 2) * SSM_HEAD_DIM]
            y_pair = y_pair + xs[:, ps] * dskip_ref[:, ps]
            y_ref[:, ps] = y_pair.astype(BF16)
        states = _dot_tn(bg, xdec_b[:, gs])
        hstate[:, gs] = hg * ecs_w[t - 1:t, gs] + states

    @pl.when(c == 0)
    def _():
        meta_tail[...] = xbuf[0:SUBLANES, :]
        meta_state[...] = hstate[...]


def _ssd(xbc, dt, dtt, cw, cb, dtb, alog, dskip, n_chunks):
    rows = xbc.shape[0]
    n_steps = rows // CHUNK - META_CHUNK

    expand = np.kron(np.eye(SSM_HEADS, dtype=np.float32), np.ones((1, SSM_HEAD_DIM), np.float32))
    consts = (
        cw, cb.reshape(1, CONV_DIM), dtb.reshape(1, SSM_HEADS), dtb.reshape(SSM_HEADS, 1),
        alog.reshape(1, SSM_HEADS), alog.reshape(SSM_HEADS, 1),
        jnp.repeat(dskip, SSM_HEAD_DIM).reshape(1, D_INNER), jnp.asarray(expand, BF16),
    )
    return pl.pallas_call(
        functools.partial(_ssd_kernel, chunks_per_batch=n_chunks),
        grid=(n_steps,),
        in_specs=[
            pl.BlockSpec((CHUNK, CONV_DIM), lambda c: (c + META_CHUNK, 0)),
            pl.BlockSpec((CHUNK, SSM_HEADS), lambda c: (c + META_CHUNK, 0)),
            pl.BlockSpec((SSM_HEADS, CHUNK), lambda c: (0, c + META_CHUNK)),
        ] + [_const_spec(a.shape) for a in consts],
        out_specs=pl.BlockSpec((CHUNK, D_INNER), lambda c: (c + META_CHUNK, 0)),
        out_shape=jax.ShapeDtypeStruct((rows, D_INNER), BF16),
        scratch_shapes=[
            pltpu.VMEM((CHUNK + SUBLANES, CONV_DIM), F32),
            pltpu.VMEM((D_STATE, D_INNER), F32),
            pltpu.VMEM((SUBLANES, CONV_DIM), F32),
            pltpu.VMEM((D_STATE, D_INNER), F32),
        ],
        compiler_params=pltpu.CompilerParams(
            dimension_semantics=("arbitrary",), vmem_limit_bytes=48 * 1024 * 1024),
        name="ssd",
    )(xbc, dt, dtt, *consts)


def _attn_kernel(q_ref, kc_ref, kp_ref, km_ref, vc_ref, vp_ref, vm_ref, bband_ref, bmeta_ref, o_ref):
    t = CHUNK
    rows = ATTN_GROUP * t
    lane = lax.broadcasted_iota(jnp.int32, (t, LANES), 1)
    low = lane < ATTN_HEAD_DIM
    ri = lax.broadcasted_iota(jnp.int32, (rows, t), 0) & (t - 1)
    ci = lax.broadcasted_iota(jnp.int32, (rows, t), 1)
    from_prev = ci > ri
    for k in range(ATTN_KV_HEADS):
        parts = []
        for half in range(ATTN_GROUP // 2):
            qp = q_ref[:, (2 * k + half) * LANES:(2 * k + half + 1) * LANES]
            zero = jnp.zeros_like(qp)
            parts += [jnp.where(low, qp, zero), jnp.where(low, zero, qp)]
        qs = jnp.concatenate(parts, axis=0)
        ks = slice(k * LANES, (k + 1) * LANES)
        s_cur = _dot_nt(qs, kc_ref[:, ks])
        s_prev = _dot_nt(qs, kp_ref[:, ks])
        s_meta = _dot_nt(qs, km_ref[:, ks])
        band = jnp.where(from_prev, s_prev, s_cur) + bband_ref[0, k]
        meta = s_meta + bmeta_ref[0, k]
        m = jnp.maximum(jnp.max(band, axis=-1, keepdims=True), jnp.max(meta, axis=-1, keepdims=True))
        p_band = jnp.exp(band - m)
        p_meta = jnp.exp(meta - m)
        den = jnp.sum(p_band, axis=-1, keepdims=True) + jnp.sum(p_meta, axis=-1, keepdims=True)
        pb = p_band.astype(BF16)
        zero = jnp.zeros_like(pb)
        o = (_dot(jnp.where(from_prev, pb, zero), vp_ref[:, ks])
             + _dot(jnp.where(from_prev, zero, pb), vc_ref[:, ks])
             + _dot(p_meta.astype(BF16), vm_ref[:, ks]))
        o = o / den
        for half in range(ATTN_GROUP // 2):
            oa = o[(2 * half) * t:(2 * half + 1) * t]
            ob = o[(2 * half + 1) * t:(2 * half + 2) * t]
            o_ref[:, (2 * k + half) * LANES:(2 * k + half + 1) * LANES] = jnp.where(low, oa, ob).astype(BF16)


def _attn_bias_tables(sinks):
    t = CHUNK
    slopes = np.exp2(-8.0 * np.arange(1, ATTN_HEADS + 1, dtype=np.float32) / ATTN_HEADS)
    i = np.arange(t)[:, None]
    j = np.arange(t)[None, :]
    prev = j > i
    dist = np.where(prev, t + i - j, i - j).astype(np.float32)
    band = np.full((3, ATTN_HEADS, t, t), NEG, np.float32)
    alibi = -slopes[:, None, None] * dist[None]
    band[1] = np.where(prev[None], NEG, alibi)
    band[2] = alibi
    meta = np.full((3, t, t), NEG, np.float32)
    is_meta_key = j >= t - N_META_TOK
    meta[0] = np.where(np.logical_and(is_meta_key, j <= i), 0.0, NEG)
    meta[1] = np.where(is_meta_key, 0.0, NEG)
    meta[2] = meta[1]
    meta = np.broadcast_to(meta[:, None], (3, ATTN_HEADS, t, t))
    sink_col = (np.arange(t) == 0)[None, None, None, :]
    meta = jnp.where(sink_col, sinks.astype(F32)[None, :, None, None], jnp.asarray(meta))
    shape = (3, ATTN_KV_HEADS, ATTN_GROUP * t, t)
    return jnp.asarray(band).reshape(shape), meta.reshape(shape)


def _attn(q, kd, vd, sinks, chunks_per_batch):
    rows = q.shape[0]
    n_steps = rows // CHUNK - META_CHUNK
    bband, bmeta = _attn_bias_tables(sinks)

    def case_of(n):
        return jnp.where(n == 0, 0, jnp.where((n - 1) % chunks_per_batch == 0, 1, 2))

    cur = lambda w: pl.BlockSpec((CHUNK, w), lambda n: (n + META_CHUNK, 0))
    prev = lambda w: pl.BlockSpec((CHUNK, w), lambda n: (n + META_CHUNK - 1, 0))
    metab = lambda w: pl.BlockSpec((CHUNK, w), lambda n: (META_CHUNK, 0))
    table = pl.BlockSpec((1,) + bband.shape[1:], lambda n: (case_of(n), 0, 0, 0))
    return pl.pallas_call(
        _attn_kernel,
        grid=(n_steps,),
        in_specs=[cur(ATTN_WIDTH), cur(KV_DUP_WIDTH), prev(KV_DUP_WIDTH), metab(KV_DUP_WIDTH),
                  cur(KV_DUP_WIDTH), prev(KV_DUP_WIDTH), metab(KV_DUP_WIDTH), table, table],
        out_specs=cur(ATTN_WIDTH),
        out_shape=jax.ShapeDtypeStruct((rows, ATTN_WIDTH), BF16),
        compiler_params=pltpu.CompilerParams(
            dimension_semantics=("arbitrary",), vmem_limit_bytes=32 * 1024 * 1024),
        name="attn",
    )(q, kd, kd, kd, vd, vd, vd, bband, bmeta)


def _merge_kernel(tok_ref, y_ref, z_ref, ya_ref, g_ref, nssm_ref, wso_ref, wao_ref, wmix_ref, npost_ref, npre_ref,
                  h1_ref, hn_ref):
    z = z_ref[...].astype(F32)
    u = y_ref[...].astype(F32) * (z * _sigmoid(z))
    y_ssm = _dot(_rms(u, nssm_ref[...]).astype(BF16), wso_ref[...])
    y_attn = _dot(ya_ref[...], wao_ref[...])
    gates = _sigmoid(g_ref[...].astype(F32))
    mixed = gates[:, :D_MODEL] * y_ssm + gates[:, D_MODEL:] * y_attn
    mix = _dot(mixed.astype(BF16), wmix_ref[...])
    h1 = tok_ref[...] + _rms(mix, npost_ref[...])
    h1_ref[...] = h1
    hn_ref[...] = _rms(h1, npre_ref[...]).astype(BF16)


def _merge(tok, y, z, ya, g, nssm, wso, wao, wmix, npost, npre):
    rows = tok.shape[0]
    tm = TOKEN_TILE
    row_spec = lambda n: pl.BlockSpec((tm, n), lambda i: (i, 0))
    consts = (nssm, wso, wao, wmix, npost, npre)
    return pl.pallas_call(
        _merge_kernel,
        grid=(rows // tm,),
        in_specs=[row_spec(D_MODEL), row_spec(D_INNER), row_spec(D_INNER), row_spec(ATTN_WIDTH),
                  row_spec(2 * D_MODEL)] + [_const_spec(a.shape) for a in consts],
        out_specs=[row_spec(D_MODEL), row_spec(D_MODEL)],
        out_shape=[jax.ShapeDtypeStruct((rows, D_MODEL), F32), jax.ShapeDtypeStruct((rows, D_MODEL), BF16)],
        compiler_params=pltpu.CompilerParams(
            dimension_semantics=("arbitrary",), vmem_limit_bytes=48 * 1024 * 1024),
        name="merge",
    )(tok, y, z, ya, g, *consts)


def _ffn_kernel(hn_ref, h1_ref, wup_ref, cw_ref, cb_ref, wdn_ref, nw_ref, o_ref, ubuf, tail, meta_tail,
                *, tiles_per_batch):
    i = pl.program_id(0)
    tm = TOKEN_TILE
    first_of_batch = (i - 1) % tiles_per_batch == 0

    @pl.when(i == 0)
    def _():
        tail[...] = jnp.zeros_like(tail)
        meta_tail[...] = jnp.zeros_like(meta_tail)

    hn = hn_ref[...]
    acc = jnp.zeros((tm, D_MODEL), F32)
    for jc in range(FFN_NCHUNK):
        u = _dot(hn, wup_ref[jc])
        ubuf[SUBLANES:SUBLANES + tm, :] = u
        ubuf[0:SUBLANES, :] = jnp.where(first_of_batch, meta_tail[jc], tail[jc])
        cw = cw_ref[jc]
        v = cb_ref[jc] + cw[FFN_CONV - 1:FFN_CONV, :] * u
        for k in range(FFN_CONV - 1):
            off = SUBLANES - (FFN_CONV - 1) + k
            v = v + cw[k:k + 1, :] * ubuf[off:off + tm, :]
        last = ubuf[tm:tm + SUBLANES, :]
        tail[jc] = last

        @pl.when(i == 0)
        def _():
            meta_tail[jc] = last

        a = v[:, :FFN_CHUNK]
        act = (a * _sigmoid(a) * v[:, FFN_CHUNK:]).astype(BF16)
        acc = acc + _dot(act, wdn_ref[jc])
    o_ref[...] = h1_ref[...] + _rms(acc, nw_ref[...])


def _ffn(hn, h1, wup, cw, cb, wdn, nw, tiles_per_batch):
    rows = hn.shape[0]
    tm = TOKEN_TILE
    row_spec = lambda n: pl.BlockSpec((tm, n), lambda i: (i, 0))
    consts = (wup, cw, cb, wdn, nw)
    return pl.pallas_call(
        functools.partial(_ffn_kernel, tiles_per_batch=tiles_per_batch),
        grid=(rows // tm,),
        in_specs=[row_spec(D_MODEL), row_spec(D_MODEL)] + [_const_spec(a.shape) for a in consts],
        out_specs=pl.BlockSpec((tm, D_MODEL), lambda i: (jnp.maximum(i - 1, 0), 0)),
        out_shape=jax.ShapeDtypeStruct((rows - tm, D_MODEL), F32),
        scratch_shapes=[
            pltpu.VMEM((tm + SUBLANES, 2 * FFN_CHUNK), F32),
            pltpu.VMEM((FFN_NCHUNK, SUBLANES, 2 * FFN_CHUNK), F32),
            pltpu.VMEM((FFN_NCHUNK, SUBLANES, 2 * FFN_CHUNK), F32),
        ],
        compiler_params=pltpu.CompilerParams(
            dimension_semantics=("arbitrary",), vmem_limit_bytes=48 * 1024 * 1024),
        name="ffn",
    )(hn, h1, *consts)


def _ffn_chunked(a, axis):
    a = jnp.moveaxis(a, axis, -1)
    lead = a.shape[:-1]
    a = a.reshape(lead + (2, FFN_NCHUNK, FFN_CHUNK))
    a = jnp.moveaxis(a, -2, 0)
    return a.reshape((FFN_NCHUNK,) + lead + (2 * FFN_CHUNK,))


def kernel(x, meta_tokens, norm_pre_mix, w_in, ssm_conv_w, ssm_conv_b, ssm_dt_bias, ssm_a_log, ssm_d_skip,
           ssm_norm, w_ssm_out, attn_sinks, w_attn_out, w_mix_out, norm_post_mix, norm_pre_ffn, w_ffn_up,
           ffn_conv_w, ffn_conv_b, w_ffn_down, norm_post_ffn):
    batch, seq, d_model = x.shape
    assert d_model == D_MODEL and w_in.shape[0] == 1 and seq % TOKEN_TILE == 0
    assert meta_tokens.shape == (N_META_TOK, D_MODEL)
    n_chunks = seq // CHUNK

    meta_tile = jnp.concatenate(
        [jnp.zeros((TOKEN_TILE - N_META_TOK, D_MODEL), x.dtype), meta_tokens.astype(x.dtype)], axis=0)
    tok = jnp.concatenate([meta_tile, x.reshape(batch * seq, D_MODEL)], axis=0)

    w = w_in[0]
    cuts = np.cumsum([0, D_INNER, CONV_DIM, SSM_HEADS, ATTN_WIDTH, ATTN_KV_HEADS * ATTN_HEAD_DIM,
                      ATTN_KV_HEADS * ATTN_HEAD_DIM, 2 * D_MODEL])
    wz, wxbc, wdt, wq, wk, wv, wg = [w[:, cuts[n]:cuts[n + 1]] for n in range(7)]

    def dup_heads(a):
        a = a.reshape(D_MODEL, ATTN_KV_HEADS, 1, ATTN_HEAD_DIM)
        return jnp.broadcast_to(a, (D_MODEL, ATTN_KV_HEADS, 2, ATTN_HEAD_DIM)).reshape(D_MODEL, KV_DUP_WIDTH)

    row = lambda a: a.reshape(1, -1).astype(F32)
    z, xbc, dt, dtt, q, kd, vd, gates = _in_proj(
        tok, row(norm_pre_mix[0]), wz.astype(BF16), wxbc.astype(BF16), wdt.astype(BF16), wdt.T.astype(BF16),
        wq.astype(BF16), dup_heads(wk).astype(BF16), dup_heads(wv).astype(BF16), wg.astype(BF16))

    y = _ssd(xbc, dt, dtt, ssm_conv_w[0].astype(F32), ssm_conv_b[0].astype(F32), ssm_dt_bias[0].astype(F32),
             ssm_a_log[0].astype(F32), ssm_d_skip[0].astype(F32), n_chunks)
    ya = _attn(q, kd, vd, attn_sinks[0], n_chunks)
    h1, hn = _merge(tok, y, z, ya, gates, row(ssm_norm[0]), w_ssm_out[0].astype(BF16), w_attn_out[0].astype(BF16),
                    w_mix_out[0].astype(BF16), row(norm_post_mix[0]), row(norm_pre_ffn[0]))
    out = _ffn(hn, h1,
               _ffn_chunked(w_ffn_up[0], 1).astype(BF16),
               _ffn_chunked(ffn_conv_w[0], 1).astype(F32),
               _ffn_chunked(ffn_conv_b[0].reshape(1, -1), 1).astype(F32),
               w_ffn_down[0].reshape(FFN_NCHUNK, FFN_CHUNK, D_MODEL).astype(BF16),
               row(norm_post_ffn[0]), seq // TOKEN_TILE)
    return out.reshape(batch, seq, D_MODEL)
```

```python
import functools

import numpy as np
import jax
import jax.numpy as jnp
from jax import lax
from jax.experimental import pallas as pl
from jax.experimental.pallas import tpu as pltpu

D_MODEL = 1024
N_META_TOK = 16
CHUNK = 128
D_INNER = 2048
SSM_HEAD_DIM = 64
SSM_HEADS = D_INNER // SSM_HEAD_DIM
SSM_GROUPS = 4
HEADS_PER_GROUP = SSM_HEADS // SSM_GROUPS
GROUP_WIDTH = HEADS_PER_GROUP * SSM_HEAD_DIM
D_STATE = 128
SSM_CONV = 4
CONV_DIM = D_INNER + 2 * SSM_GROUPS * D_STATE
ATTN_HEADS = 16
ATTN_KV_HEADS = 4
ATTN_HEAD_DIM = 64
ATTN_GROUP = ATTN_HEADS // ATTN_KV_HEADS
ATTN_WIDTH = ATTN_HEADS * ATTN_HEAD_DIM
KV_DUP_WIDTH = ATTN_KV_HEADS * 2 * ATTN_HEAD_DIM
FFN_DIM = 2816
FFN_CONV = 3
FFN_CHUNK = 256
FFN_NCHUNK = FFN_DIM // FFN_CHUNK
EPS = 1e-6
NEG = -1e30
LOG2E = float(np.log2(np.e))

LANES = 128
SUBLANES = 8
TOKEN_TILE = 512
MERGE_SUB = 256
CHUNKS_PER_TILE = TOKEN_TILE // CHUNK
META_CHUNK = CHUNKS_PER_TILE - 1
CONV_TILES = CONV_DIM // LANES
XS_TILES = D_INNER // LANES
FFN_TILES = 2 * FFN_CHUNK // LANES
FFN_BUFS = 2

F32 = jnp.float32
BF16 = jnp.bfloat16


def _dot(a, b):
    return jnp.dot(a, b, preferred_element_type=F32)


def _dot_nt(a, b):
    return lax.dot_general(a, b, (((1,), (1,)), ((), ())), preferred_element_type=F32)


def _dot_tn(a, b):
    return lax.dot_general(a, b, (((0,), (0,)), ((), ())), preferred_element_type=F32)


def _split_bf16(a, terms):
    out = []
    r = a
    for _ in range(terms):
        t = r.astype(BF16)
        out.append(t)
        r = r - t.astype(F32)
    return out


def _rms(x, w):
    return x * lax.rsqrt(jnp.mean(x * x, axis=-1, keepdims=True) + EPS) * w


def _sigmoid(x):
    return 1.0 / (1.0 + jnp.exp2(x * (-LOG2E)))


def _softplus(x):
    return jnp.maximum(x, 0.0) + jnp.log1p(jnp.exp(-jnp.abs(x)))


MIB = 1024 * 1024
VMEM_LIMIT = {"in_proj": 56 * MIB, "ssd": 48 * MIB, "attn": 32 * MIB, "merge": 48 * MIB, "ffn": 48 * MIB}


def _params(name):
    return pltpu.CompilerParams(dimension_semantics=("arbitrary",), vmem_limit_bytes=VMEM_LIMIT[name])


def _const_spec(shape):
    nd = len(shape)
    return pl.BlockSpec(shape, lambda *_: (0,) * nd, pipeline_mode=pl.Buffered(1))


def _x_spec():
    return pl.BlockSpec((TOKEN_TILE, D_MODEL), lambda i: (jnp.maximum(i - 1, 0), 0))


def _tile_tokens(x_ref, meta_ref):
    return jnp.where(pl.program_id(0) == 0, meta_ref[...], x_ref[...])


def _in_proj_kernel(x_ref, meta_ref, nw_ref, wz_ref, wxbc_ref, wdt_ref, wdtt_ref, wq_ref, wk_ref, wv_ref, wg_ref,
                    z_ref, xbc_ref, dt_ref, dtt_ref, q_ref, k_ref, v_ref, g_ref):
    xn = _rms(_tile_tokens(x_ref, meta_ref), nw_ref[...]).astype(BF16)
    z_ref[...] = _dot(xn, wz_ref[...]).astype(BF16)
    xbc_ref[...] = _dot(xn, wxbc_ref[...]).astype(BF16)
    dt_ref[...] = _dot(xn, wdt_ref[...])
    dtt_ref[...] = _dot_nt(wdtt_ref[...], xn)
    q_ref[...] = (_dot(xn, wq_ref[...]) * (ATTN_HEAD_DIM ** -0.5 * LOG2E)).astype(BF16)
    k_ref[...] = _dot(xn, wk_ref[...]).astype(BF16)
    v_ref[...] = _dot(xn, wv_ref[...]).astype(BF16)
    g_ref[...] = _dot(xn, wg_ref[...]).astype(BF16)


def _in_proj(x, meta_tile, nw, wz, wxbc, wdt, wdtt, wq, wk, wv, wg):
    rows = x.shape[0] + TOKEN_TILE
    tm = TOKEN_TILE
    row_spec = lambda n: pl.BlockSpec((tm, n), lambda i: (i, 0))
    consts = (meta_tile, nw, wz, wxbc, wdt, wdtt, wq, wk, wv, wg)
    out_shape = [
        jax.ShapeDtypeStruct((rows, D_INNER), BF16),
        jax.ShapeDtypeStruct((rows, CONV_DIM), BF16),
        jax.ShapeDtypeStruct((rows, SSM_HEADS), F32),
        jax.ShapeDtypeStruct((SSM_HEADS, rows), F32),
        jax.ShapeDtypeStruct((rows, ATTN_WIDTH), BF16),
        jax.ShapeDtypeStruct((rows, KV_DUP_WIDTH), BF16),
        jax.ShapeDtypeStruct((rows, KV_DUP_WIDTH), BF16),
        jax.ShapeDtypeStruct((rows, 2 * D_MODEL), BF16),
    ]
    out_specs = [
        row_spec(D_INNER), row_spec(CONV_DIM), row_spec(SSM_HEADS),
        pl.BlockSpec((SSM_HEADS, tm), lambda i: (0, i)),
        row_spec(ATTN_WIDTH), row_spec(KV_DUP_WIDTH), row_spec(KV_DUP_WIDTH), row_spec(2 * D_MODEL),
    ]
    return pl.pallas_call(
        _in_proj_kernel,
        grid=(rows // tm,),
        in_specs=[_x_spec()] + [_const_spec(a.shape) for a in consts],
        out_specs=out_specs,
        out_shape=out_shape,
        compiler_params=_params("in_proj"),
        name="in_proj",
    )(x, *consts)


def _ssd_kernel(xbc_ref, dt_ref, dtt_ref, cw_ref, cb_ref, dtb_ref, dtbt_ref, alog_ref, alogt_ref,
                dskip_ref, expand_ref, y_ref, xbuf, hstate, meta_tail, meta_state, *, chunks_per_batch):
    c = pl.program_id(0)
    t = CHUNK

    @pl.when(c == 0)
    def _():
        xbuf[:, 0:SUBLANES, :] = jnp.zeros((CONV_TILES, SUBLANES, LANES), F32)
        hstate[...] = jnp.zeros_like(hstate)

    @pl.when(jnp.logical_and(c > 0, (c - 1) % chunks_per_batch == 0))
    def _():
        xbuf[:, 0:SUBLANES, :] = meta_tail[...]
        hstate[...] = meta_state[...]

    row = lax.broadcasted_iota(jnp.int32, (t, 1), 0)
    col = lax.broadcasted_iota(jnp.int32, (1, t), 1)
    live_row = jnp.logical_or(row >= t - N_META_TOK, c > 0)
    live_col = jnp.logical_or(col >= t - N_META_TOK, c > 0)
    dt = jnp.where(live_row, _softplus(dt_ref[...] + dtb_ref[...]), 0.0)
    dtt = jnp.where(live_col, _softplus(dtt_ref[...] + dtbt_ref[...]), 0.0)
    adt = dt * (-jnp.exp(alog_ref[...]) * LOG2E)
    adtt = dtt * (-jnp.exp(alogt_ref[...]) * LOG2E)

    ri = lax.broadcasted_iota(jnp.int32, (t, t), 0)
    ci = lax.broadcasted_iota(jnp.int32, (t, t), 1)
    causal = ri >= ci
    lower = jnp.where(causal, 1.0, 0.0).astype(BF16)
    upper = jnp.where(ri <= ci, 1.0, 0.0).astype(BF16)
    cs = sum(_dot(lower, p) for p in _split_bf16(adt, 3))
    cst = sum(_dot(p, upper) for p in _split_bf16(adtt, 3))

    ecs = jnp.exp2(cs)
    dec = jnp.exp2(cs[t - 1:t, :] - cs)
    expand = expand_ref[...]

    def widen(v):
        return _dot(jnp.concatenate(_split_bf16(v, 2), axis=1), expand)

    dt_w = widen(dt)
    ecs_w = widen(ecs)
    dtdec_w = widen(dt * dec)

    xs_tiles, xdt_tiles, xdec_tiles, bc_tiles = [], [], [], []
    for j in range(CONV_TILES):
        ls = slice(j * LANES, (j + 1) * LANES)
        xbuf[j, SUBLANES:SUBLANES + t, :] = xbc_ref[:, ls].astype(F32)
        u = cb_ref[:, ls] + cw_ref[SSM_CONV - 1:SSM_CONV, ls] * xbuf[j, SUBLANES:SUBLANES + t, :]
        for k in range(SSM_CONV - 1):
            off = SUBLANES - (SSM_CONV - 1) + k
            u = u + cw_ref[k:k + 1, ls] * xbuf[j, off:off + t, :]
        xbuf[j, 0:SUBLANES, :] = xbuf[j, t:t + SUBLANES, :]
        u = u * _sigmoid(u)
        if j < XS_TILES:
            xs_tiles.append(u)
            xdt_tiles.append((u * dt_w[:, ls]).astype(BF16))
            xdec_tiles.append((u * dtdec_w[:, ls]).astype(BF16))
        else:
            bc_tiles.append(u.astype(BF16))

    lane = lax.broadcasted_iota(jnp.int32, (t, LANES), 1)
    first_head = lane < SSM_HEAD_DIM
    pairs = HEADS_PER_GROUP // 2
    for g in range(SSM_GROUPS):
        gs = slice(g * GROUP_WIDTH, (g + 1) * GROUP_WIDTH)
        bg = bc_tiles[g]
        cg = bc_tiles[SSM_GROUPS + g]
        hg = hstate[:, gs]
        cb = _dot_nt(cg, bg)
        y_off = _dot(cg, hg.astype(BF16)) * ecs_w[:, gs]
        for pr in range(pairs):
            tile = g * pairs + pr
            h0 = 2 * tile
            ps = slice(tile * LANES, (tile + 1) * LANES)
            decays = []
            for h in (h0, h0 + 1):
                diff = cs[:, h:h + 1] - cst[h:h + 1, :]
                decays.append((cb * jnp.exp2(jnp.where(causal, diff, NEG))).astype(BF16))
            lhs = jnp.concatenate(decays, axis=1)
            xp = xdt_tiles[tile]
            zero = jnp.zeros_like(xp)
            rhs = jnp.concatenate([jnp.where(first_head, xp, zero), jnp.where(first_head, zero, xp)], axis=0)
            y_pair = _dot(lhs, rhs) + y_off[:, pr * LANES:(pr + 1) * LANES]
            y_pair = y_pair + xs_tiles[tile] * dskip_ref[:, ps]
            y_ref[:, ps] = y_pair.astype(BF16)
        xdec_g = jnp.concatenate(xdec_tiles[g * pairs:(g + 1) * pairs], axis=1)
        states = _dot_tn(bg, xdec_g)
        hstate[:, gs] = hg * ecs_w[t - 1:t, gs] + states

    @pl.when(c == 0)
    def _():
        meta_tail[...] = xbuf[:, 0:SUBLANES, :]
        meta_state[...] = hstate[...]


def _ssd(xbc, dt, dtt, cw, cb, dtb, alog, dskip, n_chunks):
    rows = xbc.shape[0]
    n_steps = rows // CHUNK - META_CHUNK

    expand = np.kron(np.eye(SSM_HEADS, dtype=np.float32), np.ones((1, SSM_HEAD_DIM), np.float32))
    expand = np.concatenate([expand, expand], axis=0)
    consts = (
        cw, cb.reshape(1, CONV_DIM), dtb.reshape(1, SSM_HEADS), dtb.reshape(SSM_HEADS, 1),
        alog.reshape(1, SSM_HEADS), alog.reshape(SSM_HEADS, 1),
        jnp.repeat(dskip, SSM_HEAD_DIM).reshape(1, D_INNER), jnp.asarray(expand, BF16),
    )
    return pl.pallas_call(
        functools.partial(_ssd_kernel, chunks_per_batch=n_chunks),
        grid=(n_steps,),
        in_specs=[
            pl.BlockSpec((CHUNK, CONV_DIM), lambda c: (c + META_CHUNK, 0)),
            pl.BlockSpec((CHUNK, SSM_HEADS), lambda c: (c + META_CHUNK, 0)),
            pl.BlockSpec((SSM_HEADS, CHUNK), lambda c: (0, c + META_CHUNK)),
        ] + [_const_spec(a.shape) for a in consts],
        out_specs=pl.BlockSpec((CHUNK, D_INNER), lambda c: (c + META_CHUNK, 0)),
        out_shape=jax.ShapeDtypeStruct((rows, D_INNER), BF16),
        scratch_shapes=[
            pltpu.VMEM((CONV_TILES, CHUNK + SUBLANES, LANES), F32),
            pltpu.VMEM((D_STATE, D_INNER), F32),
            pltpu.VMEM((CONV_TILES, SUBLANES, LANES), F32),
            pltpu.VMEM((D_STATE, D_INNER), F32),
        ],
        compiler_params=_params("ssd"),
        name="ssd",
    )(xbc, dt, dtt, *consts)


def _attn_kernel(q_ref, kc_ref, kp_ref, km_ref, vc_ref, vp_ref, vm_ref, bband_ref, bmeta_ref, o_ref):
    t = CHUNK
    rows = ATTN_GROUP * t
    lane = lax.broadcasted_iota(jnp.int32, (t, LANES), 1)
    low = lane < ATTN_HEAD_DIM
    ri = lax.broadcasted_iota(jnp.int32, (rows, t), 0) & (t - 1)
    ci = lax.broadcasted_iota(jnp.int32, (rows, t), 1)
    from_prev = ci > ri
    for k in range(ATTN_KV_HEADS):
        parts = []
        for half in range(ATTN_GROUP // 2):
            qp = q_ref[:, (2 * k + half) * LANES:(2 * k + half + 1) * LANES]
            zero = jnp.zeros_like(qp)
            parts += [jnp.where(low, qp, zero), jnp.where(low, zero, qp)]
        qs = jnp.concatenate(parts, axis=0)
        ks = slice(k * LANES, (k + 1) * LANES)
        s_cur = _dot_nt(qs, kc_ref[:, ks])
        s_prev = _dot_nt(qs, kp_ref[:, ks])
        s_meta = _dot_nt(qs, km_ref[:, ks])
        band = jnp.where(from_prev, s_prev, s_cur) + bband_ref[0, k]
        meta = s_meta + bmeta_ref[0, k]
        m = jnp.max(jnp.maximum(band, meta), axis=-1, keepdims=True)
        p_band = jnp.exp2(band - m)
        p_meta = jnp.exp2(meta - m)
        den = jnp.sum(p_band + p_meta, axis=-1, keepdims=True)
        pb = p_band.astype(BF16)
        zero = jnp.zeros_like(pb)
        o = (_dot(jnp.where(from_prev, pb, zero), vp_ref[:, ks])
             + _dot(jnp.where(from_prev, zero, pb), vc_ref[:, ks])
             + _dot(p_meta.astype(BF16), vm_ref[:, ks]))
        o = o / den
        for half in range(ATTN_GROUP // 2):
            oa = o[(2 * half) * t:(2 * half + 1) * t]
            ob = o[(2 * half + 1) * t:(2 * half + 2) * t]
            o_ref[:, (2 * k + half) * LANES:(2 * k + half + 1) * LANES] = jnp.where(low, oa, ob).astype(BF16)


def _attn_bias_tables(sinks):
    t = CHUNK
    slopes = np.exp2(-8.0 * np.arange(1, ATTN_HEADS + 1, dtype=np.float32) / ATTN_HEADS)
    i = np.arange(t)[:, None]
    j = np.arange(t)[None, :]
    prev = j > i
    dist = np.where(prev, t + i - j, i - j).astype(np.float32)
    band = np.full((3, ATTN_HEADS, t, t), NEG, np.float32)
    alibi = (-slopes[:, None, None] * dist[None]) * np.float32(LOG2E)
    band[1] = np.where(prev[None], NEG, alibi)
    band[2] = alibi
    meta = np.full((3, t, t), NEG, np.float32)
    is_meta_key = j >= t - N_META_TOK
    meta[0] = np.where(np.logical_and(is_meta_key, j <= i), 0.0, NEG)
    meta[1] = np.where(is_meta_key, 0.0, NEG)
    meta[2] = meta[1]
    meta = np.broadcast_to(meta[:, None], (3, ATTN_HEADS, t, t))
    sink_col = (np.arange(t) == 0)[None, None, None, :]
    meta = jnp.where(sink_col, (sinks.astype(F32) * LOG2E)[None, :, None, None], jnp.asarray(meta))
    shape = (3, ATTN_KV_HEADS, ATTN_GROUP * t, t)
    return jnp.asarray(band).reshape(shape), meta.reshape(shape)


def _attn(q, kd, vd, sinks, chunks_per_batch):
    rows = q.shape[0]
    n_steps = rows // CHUNK - META_CHUNK
    bband, bmeta = _attn_bias_tables(sinks)

    def case_of(n):
        return jnp.where(n == 0, 0, jnp.where((n - 1) % chunks_per_batch == 0, 1, 2))

    cur = lambda w: pl.BlockSpec((CHUNK, w), lambda n: (n + META_CHUNK, 0))
    prev = lambda w: pl.BlockSpec((CHUNK, w), lambda n: (n + META_CHUNK - 1, 0))
    metab = lambda w: pl.BlockSpec((CHUNK, w), lambda n: (META_CHUNK, 0))
    table = pl.BlockSpec((1,) + bband.shape[1:], lambda n: (case_of(n), 0, 0, 0))
    return pl.pallas_call(
        _attn_kernel,
        grid=(n_steps,),
        in_specs=[cur(ATTN_WIDTH), cur(KV_DUP_WIDTH), prev(KV_DUP_WIDTH), metab(KV_DUP_WIDTH),
                  cur(KV_DUP_WIDTH), prev(KV_DUP_WIDTH), metab(KV_DUP_WIDTH), table, table],
        out_specs=cur(ATTN_WIDTH),
        out_shape=jax.ShapeDtypeStruct((rows, ATTN_WIDTH), BF16),
        compiler_params=_params("attn"),
        name="attn",
    )(q, kd, kd, kd, vd, vd, vd, bband, bmeta)


def _merge_kernel(x_ref, meta_ref, y_ref, z_ref, ya_ref, g_ref, nssm_ref, wso_ref, wao_ref, wmix_ref,
                  npost_ref, npre_ref, h1_ref, hn_ref):
    is_meta = pl.program_id(0) == 0
    for sb in range(TOKEN_TILE // MERGE_SUB):
        rs = slice(sb * MERGE_SUB, (sb + 1) * MERGE_SUB)
        z = z_ref[rs, :].astype(F32)
        u = y_ref[rs, :].astype(F32) * (z * _sigmoid(z))
        y_ssm = _dot(_rms(u, nssm_ref[...]).astype(BF16), wso_ref[...])
        y_attn = _dot(ya_ref[rs, :], wao_ref[...])
        gates = _sigmoid(g_ref[rs, :].astype(F32))
        mixed = gates[:, :D_MODEL] * y_ssm + gates[:, D_MODEL:] * y_attn
        mix = _dot(mixed.astype(BF16), wmix_ref[...])
        h1 = jnp.where(is_meta, meta_ref[rs, :], x_ref[rs, :]) + _rms(mix, npost_ref[...])
        h1_ref[rs, :] = h1
        hn_ref[rs, :] = _rms(h1, npre_ref[...]).astype(BF16)


def _merge(x, meta_tile, y, z, ya, g, nssm, wso, wao, wmix, npost, npre):
    rows = y.shape[0]
    tm = TOKEN_TILE
    row_spec = lambda n: pl.BlockSpec((tm, n), lambda i: (i, 0))
    consts = (nssm, wso, wao, wmix, npost, npre)
    return pl.pallas_call(
        _merge_kernel,
        grid=(rows // tm,),
        in_specs=[_x_spec(), _const_spec(meta_tile.shape), row_spec(D_INNER), row_spec(D_INNER),
                  row_spec(ATTN_WIDTH), row_spec(2 * D_MODEL)] + [_const_spec(a.shape) for a in consts],
        out_specs=[row_spec(D_MODEL), row_spec(D_MODEL)],
        out_shape=[jax.ShapeDtypeStruct((rows, D_MODEL), F32), jax.ShapeDtypeStruct((rows, D_MODEL), BF16)],
        compiler_params=_params("merge"),
        name="merge",
    )(x, meta_tile, y, z, ya, g, *consts)


def _ffn_kernel(hn_ref, h1_ref, wup_ref, cw_ref, cb_ref, wdn_ref, nw_ref, o_ref, ubuf, tail, meta_tail,
                *, tiles_per_batch):
    i = pl.program_id(0)
    tm = TOKEN_TILE
    is_meta = i == 0
    first_of_batch = (i - 1) % tiles_per_batch == 0

    @pl.when(is_meta)
    def _():
        tail[...] = jnp.zeros_like(tail)
        meta_tail[...] = jnp.zeros_like(meta_tail)

    hn = hn_ref[...]
    acc = jnp.zeros((tm, D_MODEL), F32)
    for jc in range(FFN_NCHUNK):
        u = _dot(hn, wup_ref[jc])
        buf = jc % FFN_BUFS
        v_tiles = []
        for j in range(FFN_TILES):
            ls = slice(j * LANES, (j + 1) * LANES)
            uj = u[:, ls]
            ubuf[buf, j, SUBLANES:SUBLANES + tm, :] = uj
            ubuf[buf, j, 0:SUBLANES, :] = jnp.where(first_of_batch, meta_tail[jc, j], tail[jc, j])
            v = cb_ref[jc, :, ls] + cw_ref[jc, FFN_CONV - 1:FFN_CONV, ls] * uj
            for k in range(FFN_CONV - 1):
                off = SUBLANES - (FFN_CONV - 1) + k
                v = v + cw_ref[jc, k:k + 1, ls] * ubuf[buf, j, off:off + tm, :]
            last = uj[tm - SUBLANES:tm, :]
            tail[jc, j] = last
            meta_tail[jc, j] = jnp.where(is_meta, last, meta_tail[jc, j])
            v_tiles.append(v)
        a = jnp.concatenate(v_tiles[:FFN_TILES // 2], axis=1)
        gate = jnp.concatenate(v_tiles[FFN_TILES // 2:], axis=1)
        act = (a * _sigmoid(a) * gate).astype(BF16)
        acc = acc + _dot(act, wdn_ref[jc])
    o_ref[...] = h1_ref[...] + _rms(acc, nw_ref[...])


def _ffn(hn, h1, wup, cw, cb, wdn, nw, tiles_per_batch):
    rows = hn.shape[0]
    tm = TOKEN_TILE
    row_spec = lambda n: pl.BlockSpec((tm, n), lambda i: (i, 0))
    consts = (wup, cw, cb, wdn, nw)
    halo = (FFN_NCHUNK, FFN_TILES, SUBLANES, LANES)
    return pl.pallas_call(
        functools.partial(_ffn_kernel, tiles_per_batch=tiles_per_batch),
        grid=(rows // tm,),
        in_specs=[row_spec(D_MODEL), row_spec(D_MODEL)] + [_const_spec(a.shape) for a in consts],
        out_specs=pl.BlockSpec((tm, D_MODEL), lambda i: (jnp.maximum(i - 1, 0), 0)),
        out_shape=jax.ShapeDtypeStruct((rows - tm, D_MODEL), F32),
        scratch_shapes=[
            pltpu.VMEM((FFN_BUFS, FFN_TILES, tm + SUBLANES, LANES), F32),
            pltpu.VMEM(halo, F32),
            pltpu.VMEM(halo, F32),
        ],
        compiler_params=_params("ffn"),
        name="ffn",
    )(hn, h1, *consts)


def _ffn_chunked(a, axis):
    a = jnp.moveaxis(a, axis, -1)
    lead = a.shape[:-1]
    a = a.reshape(lead + (2, FFN_NCHUNK, FFN_CHUNK))
    a = jnp.moveaxis(a, -2, 0)
    return a.reshape((FFN_NCHUNK,) + lead + (2 * FFN_CHUNK,))


def kernel(x, meta_tokens, norm_pre_mix, w_in, ssm_conv_w, ssm_conv_b, ssm_dt_bias, ssm_a_log, ssm_d_skip,
           ssm_norm, w_ssm_out, attn_sinks, w_attn_out, w_mix_out, norm_post_mix, norm_pre_ffn, w_ffn_up,
           ffn_conv_w, ffn_conv_b, w_ffn_down, norm_post_ffn):
    batch, seq, d_model = x.shape
    assert d_model == D_MODEL and w_in.shape[0] == 1 and seq % TOKEN_TILE == 0
    assert meta_tokens.shape == (N_META_TOK, D_MODEL)
    n_chunks = seq // CHUNK

    meta_tile = jnp.concatenate(
        [jnp.zeros((TOKEN_TILE - N_META_TOK, D_MODEL), x.dtype), meta_tokens.astype(x.dtype)], axis=0)
    x_flat = x.reshape(batch * seq, D_MODEL)

    w = w_in[0]
    cuts = np.cumsum([0, D_INNER, CONV_DIM, SSM_HEADS, ATTN_WIDTH, ATTN_KV_HEADS * ATTN_HEAD_DIM,
                      ATTN_KV_HEADS * ATTN_HEAD_DIM, 2 * D_MODEL])
    wz, wxbc, wdt, wq, wk, wv, wg = [w[:, cuts[n]:cuts[n + 1]] for n in range(7)]

    def dup_heads(a):
        a = a.reshape(D_MODEL, ATTN_KV_HEADS, 1, ATTN_HEAD_DIM)
        return jnp.broadcast_to(a, (D_MODEL, ATTN_KV_HEADS, 2, ATTN_HEAD_DIM)).reshape(D_MODEL, KV_DUP_WIDTH)

    row = lambda a: a.reshape(1, -1).astype(F32)
    z, xbc, dt, dtt, q, kd, vd, gates = _in_proj(
        x_flat, meta_tile, row(norm_pre_mix[0]), wz.astype(BF16), wxbc.astype(BF16), wdt.astype(BF16),
        wdt.T.astype(BF16), wq.astype(BF16), dup_heads(wk).astype(BF16), dup_heads(wv).astype(BF16),
        wg.astype(BF16))

    y = _ssd(xbc, dt, dtt, ssm_conv_w[0].astype(F32), ssm_conv_b[0].astype(F32), ssm_dt_bias[0].astype(F32),
             ssm_a_log[0].astype(F32), ssm_d_skip[0].astype(F32), n_chunks)
    ya = _attn(q, kd, vd, attn_sinks[0], n_chunks)
    h1, hn = _merge(x_flat, meta_tile, y, z, ya, gates, row(ssm_norm[0]), w_ssm_out[0].astype(BF16),
                    w_attn_out[0].astype(BF16), w_mix_out[0].astype(BF16), row(norm_post_mix[0]),
                    row(norm_pre_ffn[0]))
    out = _ffn(hn, h1,
               _ffn_chunked(w_ffn_up[0], 1).astype(BF16),
               _ffn_chunked(ffn_conv_w[0], 1).astype(F32),
               _ffn_chunked(ffn_conv_b[0].reshape(1, -1), 1).astype(F32),
               w_ffn_down[0].reshape(FFN_NCHUNK, FFN_CHUNK, D_MODEL).astype(BF16),
               row(norm_post_ffn[0]), seq // TOKEN_TILE)
    return out.reshape(batch, seq, D_MODEL)
```

```python
import functools

import numpy as np
import jax
import jax.numpy as jnp
from jax import lax
from jax.experimental import pallas as pl
from jax.experimental.pallas import tpu as pltpu

D_MODEL = 1024
N_META_TOK = 16
CHUNK = 128
D_INNER = 2048
SSM_HEAD_DIM = 64
SSM_HEADS = D_INNER // SSM_HEAD_DIM
SSM_GROUPS = 4
HEADS_PER_GROUP = SSM_HEADS // SSM_GROUPS
GROUP_WIDTH = HEADS_PER_GROUP * SSM_HEAD_DIM
D_STATE = 128
SSM_CONV = 4
CONV_DIM = D_INNER + 2 * SSM_GROUPS * D_STATE
ATTN_HEADS = 16
ATTN_KV_HEADS = 4
ATTN_HEAD_DIM = 64
ATTN_GROUP = ATTN_HEADS // ATTN_KV_HEADS
ATTN_WIDTH = ATTN_HEADS * ATTN_HEAD_DIM
KV_DUP_WIDTH = ATTN_KV_HEADS * 2 * ATTN_HEAD_DIM
FFN_DIM = 2816
FFN_CONV = 3
FFN_CHUNK = 256
FFN_NCHUNK = FFN_DIM // FFN_CHUNK
EPS = 1e-6
NEG = -1e30
LOG2E = float(np.log2(np.e))

LANES = 128
SUBLANES = 8
TOKEN_TILE = 512
MERGE_SUB = 256
CHUNKS_PER_TILE = TOKEN_TILE // CHUNK
META_CHUNK = CHUNKS_PER_TILE - 1
CONV_TILES = CONV_DIM // LANES
XS_TILES = D_INNER // LANES
FFN_TILES = 2 * FFN_CHUNK // LANES
FFN_BUFS = 2
MIB = 1024 * 1024
VMEM_LIMIT = {"zg_proj": 40 * MIB, "front": 62 * MIB, "merge": 48 * MIB, "ffn": 48 * MIB}

F32 = jnp.float32
BF16 = jnp.bfloat16


def _dot(a, b):
    return jnp.dot(a, b, preferred_element_type=F32)


def _dot_nt(a, b):
    return lax.dot_general(a, b, (((1,), (1,)), ((), ())), preferred_element_type=F32)


def _dot_tn(a, b):
    return lax.dot_general(a, b, (((0,), (0,)), ((), ())), preferred_element_type=F32)


def _split_bf16(a, terms):
    out = []
    r = a
    for _ in range(terms):
        t = r.astype(BF16)
        out.append(t)
        r = r - t.astype(F32)
    return out


def _rms(x, w):
    return x * lax.rsqrt(jnp.mean(x * x, axis=-1, keepdims=True) + EPS) * w


def _sigmoid(x):
    return 1.0 / (1.0 + jnp.exp2(x * (-LOG2E)))


def _softplus(x):
    return jnp.maximum(x, 0.0) + jnp.log1p(jnp.exp(-jnp.abs(x)))


def _params(name):
    return pltpu.CompilerParams(dimension_semantics=("arbitrary",), vmem_limit_bytes=VMEM_LIMIT[name])


def _const_spec(shape):
    nd = len(shape)
    return pl.BlockSpec(shape, lambda *_: (0,) * nd, pipeline_mode=pl.Buffered(1))


def _x_spec(n_x_tiles):
    return pl.BlockSpec((TOKEN_TILE, D_MODEL), lambda i: (jnp.clip(i - 1, 0, n_x_tiles - 1), 0))


def _tile_tokens(x_ref, meta_ref):
    return jnp.where(pl.program_id(0) == 0, meta_ref[...], x_ref[...])


def _zg_proj_kernel(x_ref, meta_ref, nw_ref, wz_ref, wg_ref, z_ref, g_ref):
    xn = _rms(_tile_tokens(x_ref, meta_ref), nw_ref[...]).astype(BF16)
    z_ref[...] = _dot(xn, wz_ref[...]).astype(BF16)
    g_ref[...] = _dot(xn, wg_ref[...]).astype(BF16)


def _zg_proj(x, meta_tile, nw, wz, wg):
    n_x_tiles = x.shape[0] // TOKEN_TILE
    rows = x.shape[0] + TOKEN_TILE
    row_spec = lambda n: pl.BlockSpec((TOKEN_TILE, n), lambda i: (i, 0))
    consts = (meta_tile, nw, wz, wg)
    return pl.pallas_call(
        _zg_proj_kernel,
        grid=(n_x_tiles + 1,),
        in_specs=[_x_spec(n_x_tiles)] + [_const_spec(a.shape) for a in consts],
        out_specs=[row_spec(D_INNER), row_spec(2 * D_MODEL)],
        out_shape=[jax.ShapeDtypeStruct((rows, D_INNER), BF16), jax.ShapeDtypeStruct((rows, 2 * D_MODEL), BF16)],
        compiler_params=_params("zg_proj"),
        name="zg_proj",
    )(x, *consts)


def _ssd_chunk(gc, ch, xbc_ref, dt_ref, dtt_ref, cw_ref, cb_ref, dtb_ref, dtbt_ref, alog_ref, alogt_ref,
               dskip_ref, expand_ref, y_ref, xbuf, hstate, meta_tail, meta_state, chunks_per_batch):
    t = CHUNK
    is_meta = gc == META_CHUNK
    restore = jnp.logical_and(gc > META_CHUNK, (gc - META_CHUNK - 1) % chunks_per_batch == 0) if ch == 0 else None

    row = lax.broadcasted_iota(jnp.int32, (t, 1), 0)
    col = lax.broadcasted_iota(jnp.int32, (1, t), 1)
    live_row = jnp.logical_or(gc > META_CHUNK, jnp.logical_and(is_meta, row >= t - N_META_TOK))
    live_col = jnp.logical_or(gc > META_CHUNK, jnp.logical_and(is_meta, col >= t - N_META_TOK))
    dt = jnp.where(live_row, _softplus(dt_ref[...] + dtb_ref[...]), 0.0)
    dtt = jnp.where(live_col, _softplus(dtt_ref[...] + dtbt_ref[...]), 0.0)
    adt = dt * (-jnp.exp(alog_ref[...]) * LOG2E)
    adtt = dtt * (-jnp.exp(alogt_ref[...]) * LOG2E)

    ri = lax.broadcasted_iota(jnp.int32, (t, t), 0)
    ci = lax.broadcasted_iota(jnp.int32, (t, t), 1)
    causal = ri >= ci
    lower = jnp.where(causal, 1.0, 0.0).astype(BF16)
    upper = jnp.where(ri <= ci, 1.0, 0.0).astype(BF16)
    cs = sum(_dot(lower, p) for p in _split_bf16(adt, 3))
    cst = sum(_dot(p, upper) for p in _split_bf16(adtt, 3))

    ecs = jnp.exp2(cs)
    dec = jnp.exp2(cs[t - 1:t, :] - cs)
    expand = expand_ref[...]

    def widen(v):
        return _dot(jnp.concatenate(_split_bf16(v, 2), axis=1), expand)

    dt_w = widen(dt)
    ecs_w = widen(ecs)
    dtdec_w = widen(dt * dec)

    xs_tiles, xdt_tiles, xdec_tiles, bc_tiles = [], [], [], []
    for j in range(CONV_TILES):
        ls = slice(j * LANES, (j + 1) * LANES)
        if ch == 0:
            xbuf[j, 0:SUBLANES, :] = jnp.where(restore, meta_tail[j], xbuf[j, 0:SUBLANES, :])
        xbuf[j, SUBLANES:SUBLANES + t, :] = xbc_ref[:, ls].astype(F32)
        u = cb_ref[:, ls] + cw_ref[SSM_CONV - 1:SSM_CONV, ls] * xbuf[j, SUBLANES:SUBLANES + t, :]
        for k in range(SSM_CONV - 1):
            off = SUBLANES - (SSM_CONV - 1) + k
            u = u + cw_ref[k:k + 1, ls] * xbuf[j, off:off + t, :]
        last = xbuf[j, t:t + SUBLANES, :]
        xbuf[j, 0:SUBLANES, :] = last
        if ch == META_CHUNK:
            meta_tail[j] = jnp.where(is_meta, last, meta_tail[j])
        u = u * _sigmoid(u)
        if j < XS_TILES:
            xs_tiles.append(u)
            xdt_tiles.append((u * dt_w[:, ls]).astype(BF16))
            xdec_tiles.append((u * dtdec_w[:, ls]).astype(BF16))
        else:
            bc_tiles.append(u.astype(BF16))

    lane = lax.broadcasted_iota(jnp.int32, (t, LANES), 1)
    first_head = lane < SSM_HEAD_DIM
    pairs = HEADS_PER_GROUP // 2
    for g in range(SSM_GROUPS):
        gs = slice(g * GROUP_WIDTH, (g + 1) * GROUP_WIDTH)
        bg = bc_tiles[g]
        cg = bc_tiles[SSM_GROUPS + g]
        hg = hstate[:, gs]
        if ch == 0:
            hg = jnp.where(restore, meta_state[:, gs], hg)
        cb = _dot_nt(cg, bg)
        y_off = _dot(cg, hg.astype(BF16)) * ecs_w[:, gs]
        for pr in range(pairs):
            tile = g * pairs + pr
            h0 = 2 * tile
            ps = slice(tile * LANES, (tile + 1) * LANES)
            decays = []
            for h in (h0, h0 + 1):
                diff = cs[:, h:h + 1] - cst[h:h + 1, :]
                decays.append((cb * jnp.exp2(jnp.where(causal, diff, NEG))).astype(BF16))
            lhs = jnp.concatenate(decays, axis=1)
            xp = xdt_tiles[tile]
            zero = jnp.zeros_like(xp)
            rhs = jnp.concatenate([jnp.where(first_head, xp, zero), jnp.where(first_head, zero, xp)], axis=0)
            y_pair = _dot(lhs, rhs) + y_off[:, pr * LANES:(pr + 1) * LANES]
            y_pair = y_pair + xs_tiles[tile] * dskip_ref[:, ps]
            y_ref[:, ps] = y_pair.astype(BF16)
        xdec_g = jnp.concatenate(xdec_tiles[g * pairs:(g + 1) * pairs], axis=1)
        states = _dot_tn(bg, xdec_g)
        h_out = hg * ecs_w[t - 1:t, gs] + states
        hstate[:, gs] = h_out
        if ch == META_CHUNK:
            meta_state[:, gs] = jnp.where(is_meta, h_out, meta_state[:, gs])


def _ssd_consts(cw, cb, dtb, alog, dskip):
    expand = np.kron(np.eye(SSM_HEADS, dtype=np.float32), np.ones((1, SSM_HEAD_DIM), np.float32))
    expand = np.concatenate([expand, expand], axis=0)
    return (
        cw, cb.reshape(1, CONV_DIM), dtb.reshape(1, SSM_HEADS), dtb.reshape(SSM_HEADS, 1),
        alog.reshape(1, SSM_HEADS), alog.reshape(SSM_HEADS, 1),
        jnp.repeat(dskip, SSM_HEAD_DIM).reshape(1, D_INNER), jnp.asarray(expand, BF16),
    )


def _attn_chunk(q_ref, kc_ref, kp_ref, km_ref, vc_ref, vp_ref, vm_ref, bband_ref, bmeta_ref, o_ref):
    t = CHUNK
    rows = ATTN_GROUP * t
    lane = lax.broadcasted_iota(jnp.int32, (t, LANES), 1)
    low = lane < ATTN_HEAD_DIM
    ri = lax.broadcasted_iota(jnp.int32, (rows, t), 0) & (t - 1)
    ci = lax.broadcasted_iota(jnp.int32, (rows, t), 1)
    from_prev = ci > ri
    for k in range(ATTN_KV_HEADS):
        parts = []
        for half in range(ATTN_GROUP // 2):
            qp = q_ref[:, (2 * k + half) * LANES:(2 * k + half + 1) * LANES]
            zero = jnp.zeros_like(qp)
            parts += [jnp.where(low, qp, zero), jnp.where(low, zero, qp)]
        qs = jnp.concatenate(parts, axis=0)
        ks = slice(k * LANES, (k + 1) * LANES)
        s_cur = _dot_nt(qs, kc_ref[:, ks])
        s_prev = _dot_nt(qs, kp_ref[:, ks])
        s_meta = _dot_nt(qs, km_ref[:, ks])
        band = jnp.where(from_prev, s_prev, s_cur) + bband_ref[k]
        meta = s_meta + bmeta_ref[k]
        m = jnp.max(jnp.maximum(band, meta), axis=-1, keepdims=True)
        p_band = jnp.exp2(band - m)
        p_meta = jnp.exp2(meta - m)
        den = jnp.sum(p_band + p_meta, axis=-1, keepdims=True)
        pb = p_band.astype(BF16)
        zero = jnp.zeros_like(pb)
        o = (_dot(jnp.where(from_prev, pb, zero), vp_ref[:, ks])
             + _dot(jnp.where(from_prev, zero, pb), vc_ref[:, ks])
             + _dot(p_meta.astype(BF16), vm_ref[:, ks]))
        o = o / den
        for half in range(ATTN_GROUP // 2):
            oa = o[(2 * half) * t:(2 * half + 1) * t]
            ob = o[(2 * half + 1) * t:(2 * half + 2) * t]
            o_ref[:, (2 * k + half) * LANES:(2 * k + half + 1) * LANES] = jnp.where(low, oa, ob).astype(BF16)


def _attn_bias_tables(sinks):
    t = CHUNK
    slopes = np.exp2(-8.0 * np.arange(1, ATTN_HEADS + 1, dtype=np.float32) / ATTN_HEADS)
    i = np.arange(t)[:, None]
    j = np.arange(t)[None, :]
    prev = j > i
    dist = np.where(prev, t + i - j, i - j).astype(np.float32)
    band = np.full((3, ATTN_HEADS, t, t), NEG, np.float32)
    alibi = (-slopes[:, None, None] * dist[None]) * np.float32(LOG2E)
    band[1] = np.where(prev[None], NEG, alibi)
    band[2] = alibi
    meta = np.full((3, t, t), NEG, np.float32)
    is_meta_key = j >= t - N_META_TOK
    meta[0] = np.where(np.logical_and(is_meta_key, j <= i), 0.0, NEG)
    meta[1] = np.where(is_meta_key, 0.0, NEG)
    meta[2] = meta[1]
    meta = np.broadcast_to(meta[:, None], (3, ATTN_HEADS, t, t))
    sink_col = (np.arange(t) == 0)[None, None, None, :]
    meta = jnp.where(sink_col, (sinks.astype(F32) * LOG2E)[None, :, None, None], jnp.asarray(meta))
    shape = (3, ATTN_KV_HEADS, ATTN_GROUP * t, t)
    return jnp.asarray(band).reshape(shape), meta.reshape(shape)


N_PROJ_IN = 9
N_SSD_CONST = 8


def _front_kernel(*refs, n_tiles, chunks_per_batch):
    x_ref, meta_ref, nw_ref, wxbc_ref, wdt_ref, wdtt_ref, wq_ref, wk_ref, wv_ref = refs[:N_PROJ_IN]
    ssd_consts = refs[N_PROJ_IN:N_PROJ_IN + N_SSD_CONST]
    bband_ref, bmeta_ref, y_ref, ya_ref = refs[N_PROJ_IN + N_SSD_CONST:N_PROJ_IN + N_SSD_CONST + 4]
    (xbc_s, dt_s, dtt_s, q_s, k_s, v_s, k_carry, v_carry, k_meta, v_meta,
     xbuf, hstate, meta_tail, meta_state) = refs[N_PROJ_IN + N_SSD_CONST + 4:]
    i = pl.program_id(0)
    slot = i % 2

    def project():
        xn = _rms(_tile_tokens(x_ref, meta_ref), nw_ref[...]).astype(BF16)
        xbc_s[slot] = _dot(xn, wxbc_ref[...]).astype(BF16)
        dt_s[slot] = _dot(xn, wdt_ref[...])
        dtt_s[slot] = _dot_nt(wdtt_ref[...], xn)
        q_s[slot] = (_dot(xn, wq_ref[...]) * (ATTN_HEAD_DIM ** -0.5 * LOG2E)).astype(BF16)
        k_s[slot] = _dot(xn, wk_ref[...]).astype(BF16)
        v_s[slot] = _dot(xn, wv_ref[...]).astype(BF16)

    @pl.when(i == 0)
    def _():
        for ref in (k_carry, v_carry, k_meta, v_meta, hstate, meta_tail, meta_state):
            ref[...] = jnp.zeros_like(ref)
        xbuf[:, 0:SUBLANES, :] = jnp.zeros((CONV_TILES, SUBLANES, LANES), F32)
        project()

    @pl.when(i > 0)
    def _():
        project()
        prev_slot = 1 - slot
        tile = i - 1
        for ch in range(CHUNKS_PER_TILE):
            gc = tile * CHUNKS_PER_TILE + ch
            rs = pl.ds(ch * CHUNK, CHUNK)
            _ssd_chunk(gc, ch, xbc_s.at[prev_slot, rs], dt_s.at[prev_slot, rs], dtt_s.at[prev_slot, :, rs],
                       *ssd_consts, y_ref.at[rs], xbuf, hstate, meta_tail, meta_state, chunks_per_batch)
            case = jnp.where(gc <= META_CHUNK, 0,
                             jnp.where((gc - META_CHUNK - 1) % chunks_per_batch == 0, 1, 2))
            if ch == 0:
                kp_ref, vp_ref = k_carry, v_carry
            else:
                ps = pl.ds((ch - 1) * CHUNK, CHUNK)
                kp_ref, vp_ref = k_s.at[prev_slot, ps], v_s.at[prev_slot, ps]
            if ch == META_CHUNK:
                k_meta[...] = jnp.where(tile == 0, k_s[prev_slot, rs], k_meta[...])
                v_meta[...] = jnp.where(tile == 0, v_s[prev_slot, rs], v_meta[...])
            _attn_chunk(q_s.at[prev_slot, rs], k_s.at[prev_slot, rs], kp_ref, k_meta,
                        v_s.at[prev_slot, rs], vp_ref, v_meta, bband_ref.at[case], bmeta_ref.at[case],
                        ya_ref.at[rs])
            if ch == CHUNKS_PER_TILE - 1:
                k_carry[...] = k_s[prev_slot, rs]
                v_carry[...] = v_s[prev_slot, rs]


def _front(x, meta_tile, nw, wxbc, wdt, wdtt, wq, wk, wv, ssd_consts, sinks, chunks_per_batch):
    n_x_tiles = x.shape[0] // TOKEN_TILE
    n_tiles = n_x_tiles + 1
    rows = n_tiles * TOKEN_TILE
    bband, bmeta = _attn_bias_tables(sinks)
    consts = (meta_tile, nw, wxbc, wdt, wdtt, wq, wk, wv) + tuple(ssd_consts) + (bband, bmeta)
    out_spec = lambda n: pl.BlockSpec((TOKEN_TILE, n), lambda i: (jnp.maximum(i - 1, 0), 0))
    tm = TOKEN_TILE
    return pl.pallas_call(
        functools.partial(_front_kernel, n_tiles=n_tiles, chunks_per_batch=chunks_per_batch),
        grid=(n_tiles + 1,),
        in_specs=[_x_spec(n_x_tiles)] + [_const_spec(a.shape) for a in consts],
        out_specs=[out_spec(D_INNER), out_spec(ATTN_WIDTH)],
        out_shape=[jax.ShapeDtypeStruct((rows, D_INNER), BF16), jax.ShapeDtypeStruct((rows, ATTN_WIDTH), BF16)],
        scratch_shapes=[
            pltpu.VMEM((2, tm, CONV_DIM), BF16),
            pltpu.VMEM((2, tm, SSM_HEADS), F32),
            pltpu.VMEM((2, SSM_HEADS, tm), F32),
            pltpu.VMEM((2, tm, ATTN_WIDTH), BF16),
            pltpu.VMEM((2, tm, KV_DUP_WIDTH), BF16),
            pltpu.VMEM((2, tm, KV_DUP_WIDTH), BF16),
            pltpu.VMEM((CHUNK, KV_DUP_WIDTH), BF16),
            pltpu.VMEM((CHUNK, KV_DUP_WIDTH), BF16),
            pltpu.VMEM((CHUNK, KV_DUP_WIDTH), BF16),
            pltpu.VMEM((CHUNK, KV_DUP_WIDTH), BF16),
            pltpu.VMEM((CONV_TILES, CHUNK + SUBLANES, LANES), F32),
            pltpu.VMEM((D_STATE, D_INNER), F32),
            pltpu.VMEM((CONV_TILES, SUBLANES, LANES), F32),
            pltpu.VMEM((D_STATE, D_INNER), F32),
        ],
        compiler_params=_params("front"),
        name="front",
    )(x, *consts)


def _merge_kernel(x_ref, meta_ref, y_ref, z_ref, ya_ref, g_ref, nssm_ref, wso_ref, wao_ref, wmix_ref,
                  npost_ref, npre_ref, h1_ref, hn_ref):
    is_meta = pl.program_id(0) == 0
    for sb in range(TOKEN_TILE // MERGE_SUB):
        rs = slice(sb * MERGE_SUB, (sb + 1) * MERGE_SUB)
        z = z_ref[rs, :].astype(F32)
        u = y_ref[rs, :].astype(F32) * (z * _sigmoid(z))
        y_ssm = _dot(_rms(u, nssm_ref[...]).astype(BF16), wso_ref[...])
        y_attn = _dot(ya_ref[rs, :], wao_ref[...])
        gates = _sigmoid(g_ref[rs, :].astype(F32))
        mixed = gates[:, :D_MODEL] * y_ssm + gates[:, D_MODEL:] * y_attn
        mix = _dot(mixed.astype(BF16), wmix_ref[...])
        h1 = jnp.where(is_meta, meta_ref[rs, :], x_ref[rs, :]) + _rms(mix, npost_ref[...])
        h1_ref[rs, :] = h1
        hn_ref[rs, :] = _rms(h1, npre_ref[...]).astype(BF16)


def _merge(x, meta_tile, y, z, ya, g, nssm, wso, wao, wmix, npost, npre):
    rows = y.shape[0]
    tm = TOKEN_TILE
    row_spec = lambda n: pl.BlockSpec((tm, n), lambda i: (i, 0))
    consts = (nssm, wso, wao, wmix, npost, npre)
    return pl.pallas_call(
        _merge_kernel,
        grid=(rows // tm,),
        in_specs=[_x_spec(rows // tm - 1), _const_spec(meta_tile.shape), row_spec(D_INNER), row_spec(D_INNER),
                  row_spec(ATTN_WIDTH), row_spec(2 * D_MODEL)] + [_const_spec(a.shape) for a in consts],
        out_specs=[row_spec(D_MODEL), row_spec(D_MODEL)],
        out_shape=[jax.ShapeDtypeStruct((rows, D_MODEL), F32), jax.ShapeDtypeStruct((rows, D_MODEL), BF16)],
        compiler_params=_params("merge"),
        name="merge",
    )(x, meta_tile, y, z, ya, g, *consts)


def _ffn_kernel(hn_ref, h1_ref, wup_ref, cw_ref, cb_ref, wdn_ref, nw_ref, o_ref, ubuf, tail, meta_tail,
                *, tiles_per_batch):
    i = pl.program_id(0)
    tm = TOKEN_TILE
    is_meta = i == 0
    first_of_batch = (i - 1) % tiles_per_batch == 0

    @pl.when(is_meta)
    def _():
        tail[...] = jnp.zeros_like(tail)
        meta_tail[...] = jnp.zeros_like(meta_tail)

    hn = hn_ref[...]
    acc = jnp.zeros((tm, D_MODEL), F32)
    for jc in range(FFN_NCHUNK):
        u = _dot(hn, wup_ref[jc])
        buf = jc % FFN_BUFS
        v_tiles = []
        for j in range(FFN_TILES):
            ls = slice(j * LANES, (j + 1) * LANES)
            uj = u[:, ls]
            ubuf[buf, j, SUBLANES:SUBLANES + tm, :] = uj
            ubuf[buf, j, 0:SUBLANES, :] = jnp.where(first_of_batch, meta_tail[jc, j], tail[jc, j])
            v = cb_ref[jc, :, ls] + cw_ref[jc, FFN_CONV - 1:FFN_CONV, ls] * uj
            for k in range(FFN_CONV - 1):
                off = SUBLANES - (FFN_CONV - 1) + k
                v = v + cw_ref[jc, k:k + 1, ls] * ubuf[buf, j, off:off + tm, :]
            last = uj[tm - SUBLANES:tm, :]
            tail[jc, j] = last
            meta_tail[jc, j] = jnp.where(is_meta, last, meta_tail[jc, j])
            v_tiles.append(v)
        a = jnp.concatenate(v_tiles[:FFN_TILES // 2], axis=1)
        gate = jnp.concatenate(v_tiles[FFN_TILES // 2:], axis=1)
        act = (a * _sigmoid(a) * gate).astype(BF16)
        acc = acc + _dot(act, wdn_ref[jc])
    o_ref[...] = h1_ref[...] + _rms(acc, nw_ref[...])


def _ffn(hn, h1, wup, cw, cb, wdn, nw, tiles_per_batch):
    rows = hn.shape[0]
    tm = TOKEN_TILE
    row_spec = lambda n: pl.BlockSpec((tm, n), lambda i: (i, 0))
    consts = (wup, cw, cb, wdn, nw)
    halo = (FFN_NCHUNK, FFN_TILES, SUBLANES, LANES)
    return pl.pallas_call(
        functools.partial(_ffn_kernel, tiles_per_batch=tiles_per_batch),
        grid=(rows // tm,),
        in_specs=[row_spec(D_MODEL), row_spec(D_MODEL)] + [_const_spec(a.shape) for a in consts],
        out_specs=pl.BlockSpec((tm, D_MODEL), lambda i: (jnp.maximum(i - 1, 0), 0)),
        out_shape=jax.ShapeDtypeStruct((rows - tm, D_MODEL), F32),
        scratch_shapes=[
            pltpu.VMEM((FFN_BUFS, FFN_TILES, tm + SUBLANES, LANES), F32),
            pltpu.VMEM(halo, F32),
            pltpu.VMEM(halo, F32),
        ],
        compiler_params=_params("ffn"),
        name="ffn",
    )(hn, h1, *consts)


def _ffn_chunked(a, axis):
    a = jnp.moveaxis(a, axis, -1)
    lead = a.shape[:-1]
    a = a.reshape(lead + (2, FFN_NCHUNK, FFN_CHUNK))
    a = jnp.moveaxis(a, -2, 0)
    return a.reshape((FFN_NCHUNK,) + lead + (2 * FFN_CHUNK,))


def kernel(x, meta_tokens, norm_pre_mix, w_in, ssm_conv_w, ssm_conv_b, ssm_dt_bias, ssm_a_log, ssm_d_skip,
           ssm_norm, w_ssm_out, attn_sinks, w_attn_out, w_mix_out, norm_post_mix, norm_pre_ffn, w_ffn_up,
           ffn_conv_w, ffn_conv_b, w_ffn_down, norm_post_ffn):
    batch, seq, d_model = x.shape
    assert d_model == D_MODEL and w_in.shape[0] == 1 and seq % TOKEN_TILE == 0
    assert meta_tokens.shape == (N_META_TOK, D_MODEL)
    n_chunks = seq // CHUNK

    meta_tile = jnp.concatenate(
        [jnp.zeros((TOKEN_TILE - N_META_TOK, D_MODEL), x.dtype), meta_tokens.astype(x.dtype)], axis=0)
    x_flat = x.reshape(batch * seq, D_MODEL)

    w = w_in[0]
    cuts = np.cumsum([0, D_INNER, CONV_DIM, SSM_HEADS, ATTN_WIDTH, ATTN_KV_HEADS * ATTN_HEAD_DIM,
                      ATTN_KV_HEADS * ATTN_HEAD_DIM, 2 * D_MODEL])
    wz, wxbc, wdt, wq, wk, wv, wg = [w[:, cuts[n]:cuts[n + 1]] for n in range(7)]

    def dup_heads(a):
        a = a.reshape(D_MODEL, ATTN_KV_HEADS, 1, ATTN_HEAD_DIM)
        return jnp.broadcast_to(a, (D_MODEL, ATTN_KV_HEADS, 2, ATTN_HEAD_DIM)).reshape(D_MODEL, KV_DUP_WIDTH)

    row = lambda a: a.reshape(1, -1).astype(F32)
    nw = row(norm_pre_mix[0])
    z, gates = _zg_proj(x_flat, meta_tile, nw, wz.astype(BF16), wg.astype(BF16))
    ssd_consts = _ssd_consts(ssm_conv_w[0].astype(F32), ssm_conv_b[0].astype(F32), ssm_dt_bias[0].astype(F32),
                             ssm_a_log[0].astype(F32), ssm_d_skip[0].astype(F32))
    y, ya = _front(x_flat, meta_tile, nw, wxbc.astype(BF16), wdt.astype(BF16), wdt.T.astype(BF16),
                   wq.astype(BF16), dup_heads(wk).astype(BF16), dup_heads(wv).astype(BF16), ssd_consts,
                   attn_sinks[0], n_chunks)
    h1, hn = _merge(x_flat, meta_tile, y, z, ya, gates, row(ssm_norm[0]), w_ssm_out[0].astype(BF16),
                    w_attn_out[0].astype(BF16), w_mix_out[0].astype(BF16), row(norm_post_mix[0]),
                    row(norm_pre_ffn[0]))
    out = _ffn(hn, h1,
               _ffn_chunked(w_ffn_up[0], 1).astype(BF16),
               _ffn_chunked(ffn_conv_w[0], 1).astype(F32),
               _ffn_chunked(ffn_conv_b[0].reshape(1, -1), 1).astype(F32),
               w_ffn_down[0].reshape(FFN_NCHUNK, FFN_CHUNK, D_MODEL).astype(BF16),
               row(norm_post_ffn[0]), seq // TOKEN_TILE)
    return out.reshape(batch, seq, D_MODEL)
```

```python
import functools

import numpy as np
import jax
import jax.numpy as jnp
from jax import lax
from jax.experimental import pallas as pl
from jax.experimental.pallas import tpu as pltpu

D_MODEL = 1024
N_META_TOK = 16
CHUNK = 128
D_INNER = 2048
SSM_HEAD_DIM = 64
SSM_HEADS = D_INNER // SSM_HEAD_DIM
SSM_GROUPS = 4
HEADS_PER_GROUP = SSM_HEADS // SSM_GROUPS
GROUP_WIDTH = HEADS_PER_GROUP * SSM_HEAD_DIM
D_STATE = 128
SSM_CONV = 4
CONV_DIM = D_INNER + 2 * SSM_GROUPS * D_STATE
ATTN_HEADS = 16
ATTN_KV_HEADS = 4
ATTN_HEAD_DIM = 64
ATTN_GROUP = ATTN_HEADS // ATTN_KV_HEADS
ATTN_WIDTH = ATTN_HEADS * ATTN_HEAD_DIM
KV_DUP_WIDTH = ATTN_KV_HEADS * 2 * ATTN_HEAD_DIM
FFN_DIM = 2816
FFN_CONV = 3
FFN_CHUNK = 256
FFN_NCHUNK = FFN_DIM // FFN_CHUNK
EPS = 1e-6
NEG = -1e30
LOG2E = float(np.log2(np.e))

LANES = 128
SUBLANES = 8
TOKEN_TILE = 512
MERGE_SUB = 256
CHUNKS_PER_TILE = TOKEN_TILE // CHUNK
META_CHUNK = CHUNKS_PER_TILE - 1
CONV_TILES = CONV_DIM // LANES
XS_TILES = D_INNER // LANES
FFN_TILES = 2 * FFN_CHUNK // LANES
FFN_BUFS = 2
MIB = 1024 * 1024
VMEM_LIMIT = {"zg_proj": 40 * MIB, "front": 62 * MIB, "merge": 48 * MIB, "ffn": 48 * MIB}

F32 = jnp.float32
BF16 = jnp.bfloat16


def _dot(a, b):
    return jnp.dot(a, b, preferred_element_type=F32)


def _dot_nt(a, b):
    return lax.dot_general(a, b, (((1,), (1,)), ((), ())), preferred_element_type=F32)


def _dot_tn(a, b):
    return lax.dot_general(a, b, (((0,), (0,)), ((), ())), preferred_element_type=F32)


def _split_bf16(a, terms):
    out = []
    r = a
    for _ in range(terms):
        t = r.astype(BF16)
        out.append(t)
        r = r - t.astype(F32)
    return out


def _rms(x, w):
    return x * lax.rsqrt(jnp.mean(x * x, axis=-1, keepdims=True) + EPS) * w


def _sigmoid(x):
    return 1.0 / (1.0 + jnp.exp2(x * (-LOG2E)))


def _softplus(x):
    return jnp.maximum(x, 0.0) + jnp.log1p(jnp.exp(-jnp.abs(x)))


def _params(name):
    return pltpu.CompilerParams(dimension_semantics=("arbitrary",), vmem_limit_bytes=VMEM_LIMIT[name])


def _const_spec(shape):
    nd = len(shape)
    return pl.BlockSpec(shape, lambda *_: (0,) * nd, pipeline_mode=pl.Buffered(1))


def _x_spec(n_x_tiles):
    return pl.BlockSpec((TOKEN_TILE, D_MODEL), lambda i: (jnp.clip(i - 1, 0, n_x_tiles - 1), 0))


def _tile_tokens(x_ref, meta_ref):
    return jnp.where(pl.program_id(0) == 0, meta_ref[...], x_ref[...])


def _zg_proj_kernel(x_ref, meta_ref, nw_ref, wz_ref, wg_ref, z_ref, g_ref):
    xn = _rms(_tile_tokens(x_ref, meta_ref), nw_ref[...]).astype(BF16)
    z_ref[...] = _dot(xn, wz_ref[...]).astype(BF16)
    g_ref[...] = _dot(xn, wg_ref[...]).astype(BF16)


def _zg_proj(x, meta_tile, nw, wz, wg):
    n_x_tiles = x.shape[0] // TOKEN_TILE
    rows = x.shape[0] + TOKEN_TILE
    row_spec = lambda n: pl.BlockSpec((TOKEN_TILE, n), lambda i: (i, 0))
    consts = (meta_tile, nw, wz, wg)
    return pl.pallas_call(
        _zg_proj_kernel,
        grid=(n_x_tiles + 1,),
        in_specs=[_x_spec(n_x_tiles)] + [_const_spec(a.shape) for a in consts],
        out_specs=[row_spec(D_INNER), row_spec(2 * D_MODEL)],
        out_shape=[jax.ShapeDtypeStruct((rows, D_INNER), BF16), jax.ShapeDtypeStruct((rows, 2 * D_MODEL), BF16)],
        compiler_params=_params("zg_proj"),
        name="zg_proj",
    )(x, *consts)


def _ssd_chunk(gc, ch, xbc_ref, dt_ref, dtt_ref, cw_ref, cb_ref, dtb_ref, dtbt_ref, alog_ref, alogt_ref,
               dskip_ref, expand_ref, y_ref, xbuf, hstate, meta_tail, meta_state, chunks_per_batch):
    t = CHUNK
    is_meta = gc == META_CHUNK
    restore = jnp.logical_and(gc > META_CHUNK, (gc - META_CHUNK - 1) % chunks_per_batch == 0) if ch == 0 else None

    row = lax.broadcasted_iota(jnp.int32, (t, 1), 0)
    col = lax.broadcasted_iota(jnp.int32, (1, t), 1)
    live_row = jnp.logical_or(gc > META_CHUNK, jnp.logical_and(is_meta, row >= t - N_META_TOK))
    live_col = jnp.logical_or(gc > META_CHUNK, jnp.logical_and(is_meta, col >= t - N_META_TOK))
    dt = jnp.where(live_row, _softplus(dt_ref[...] + dtb_ref[...]), 0.0)
    dtt = jnp.where(live_col, _softplus(dtt_ref[...] + dtbt_ref[...]), 0.0)
    adt = dt * (-jnp.exp(alog_ref[...]) * LOG2E)
    adtt = dtt * (-jnp.exp(alogt_ref[...]) * LOG2E)

    ri = lax.broadcasted_iota(jnp.int32, (t, t), 0)
    ci = lax.broadcasted_iota(jnp.int32, (t, t), 1)
    causal = ri >= ci
    lower = jnp.where(causal, 1.0, 0.0).astype(BF16)
    upper = jnp.where(ri <= ci, 1.0, 0.0).astype(BF16)
    cs = sum(_dot(lower, p) for p in _split_bf16(adt, 3))
    cst = sum(_dot(p, upper) for p in _split_bf16(adtt, 3))

    ecs = jnp.exp2(cs)
    dec = jnp.exp2(cs[t - 1:t, :] - cs)
    expand = expand_ref[...]

    def widen(v):
        return _dot(jnp.concatenate(_split_bf16(v, 2), axis=1), expand)

    dt_w = widen(dt)
    ecs_w = widen(ecs)
    dtdec_w = widen(dt * dec)

    xs_tiles, xdt_tiles, xdec_tiles, bc_tiles = [], [], [], []
    for j in range(CONV_TILES):
        ls = slice(j * LANES, (j + 1) * LANES)
        if ch == 0:
            xbuf[j, 0:SUBLANES, :] = jnp.where(restore, meta_tail[j], xbuf[j, 0:SUBLANES, :])
        xbuf[j, SUBLANES:SUBLANES + t, :] = xbc_ref[:, ls].astype(F32)
        u = cb_ref[:, ls] + cw_ref[SSM_CONV - 1:SSM_CONV, ls] * xbuf[j, SUBLANES:SUBLANES + t, :]
        for k in range(SSM_CONV - 1):
            off = SUBLANES - (SSM_CONV - 1) + k
            u = u + cw_ref[k:k + 1, ls] * xbuf[j, off:off + t, :]
        last = xbuf[j, t:t + SUBLANES, :]
        xbuf[j, 0:SUBLANES, :] = last
        if ch == META_CHUNK:
            meta_tail[j] = jnp.where(is_meta, last, meta_tail[j])
        u = u * _sigmoid(u)
        if j < XS_TILES:
            xs_tiles.append(u)
            xdt_tiles.append((u * dt_w[:, ls]).astype(BF16))
            xdec_tiles.append((u * dtdec_w[:, ls]).astype(BF16))
        else:
            bc_tiles.append(u.astype(BF16))

    lane = lax.broadcasted_iota(jnp.int32, (t, LANES), 1)
    first_head = lane < SSM_HEAD_DIM
    pairs = HEADS_PER_GROUP // 2
    for g in range(SSM_GROUPS):
        gs = slice(g * GROUP_WIDTH, (g + 1) * GROUP_WIDTH)
        bg = bc_tiles[g]
        cg = bc_tiles[SSM_GROUPS + g]
        hg = hstate[:, gs]
        if ch == 0:
            hg = jnp.where(restore, meta_state[:, gs], hg)
        cb = _dot_nt(cg, bg)
        y_off = _dot(cg, hg.astype(BF16)) * ecs_w[:, gs]
        for pr in range(pairs):
            tile = g * pairs + pr
            h0 = 2 * tile
            ps = slice(tile * LANES, (tile + 1) * LANES)
            decays = []
            for h in (h0, h0 + 1):
                diff = cs[:, h:h + 1] - cst[h:h + 1, :]
                decays.append((cb * jnp.exp2(jnp.where(causal, diff, NEG))).astype(BF16))
            lhs = jnp.concatenate(decays, axis=1)
            xp = xdt_tiles[tile]
            zero = jnp.zeros_like(xp)
            rhs = jnp.concatenate([jnp.where(first_head, xp, zero), jnp.where(first_head, zero, xp)], axis=0)
            y_pair = _dot(lhs, rhs) + y_off[:, pr * LANES:(pr + 1) * LANES]
            y_pair = y_pair + xs_tiles[tile] * dskip_ref[:, ps]
            y_ref[:, ps] = y_pair.astype(BF16)
        xdec_g = jnp.concatenate(xdec_tiles[g * pairs:(g + 1) * pairs], axis=1)
        states = _dot_tn(bg, xdec_g)
        h_out = hg * ecs_w[t - 1:t, gs] + states
        hstate[:, gs] = h_out
        if ch == META_CHUNK:
            meta_state[:, gs] = jnp.where(is_meta, h_out, meta_state[:, gs])


def _ssd_consts(cw, cb, dtb, alog, dskip):
    expand = np.kron(np.eye(SSM_HEADS, dtype=np.float32), np.ones((1, SSM_HEAD_DIM), np.float32))
    expand = np.concatenate([expand, expand], axis=0)
    return (
        cw, cb.reshape(1, CONV_DIM), dtb.reshape(1, SSM_HEADS), dtb.reshape(SSM_HEADS, 1),
        alog.reshape(1, SSM_HEADS), alog.reshape(SSM_HEADS, 1),
        jnp.repeat(dskip, SSM_HEAD_DIM).reshape(1, D_INNER), jnp.asarray(expand, BF16),
    )


def _attn_chunk(q_ref, kc_ref, kp_ref, km_ref, vc_ref, vp_ref, vm_ref, bband_ref, bmeta_ref, o_ref):
    t = CHUNK
    rows = ATTN_GROUP * t
    lane = lax.broadcasted_iota(jnp.int32, (t, LANES), 1)
    low = lane < ATTN_HEAD_DIM
    ri = lax.broadcasted_iota(jnp.int32, (rows, t), 0) & (t - 1)
    ci = lax.broadcasted_iota(jnp.int32, (rows, t), 1)
    from_prev = ci > ri
    for k in range(ATTN_KV_HEADS):
        parts = []
        for half in range(ATTN_GROUP // 2):
            qp = q_ref[:, (2 * k + half) * LANES:(2 * k + half + 1) * LANES]
            zero = jnp.zeros_like(qp)
            parts += [jnp.where(low, qp, zero), jnp.where(low, zero, qp)]
        qs = jnp.concatenate(parts, axis=0)
        ks = slice(k * LANES, (k + 1) * LANES)
        s_cur = _dot_nt(qs, kc_ref[:, ks])
        s_prev = _dot_nt(qs, kp_ref[:, ks])
        s_meta = _dot_nt(qs, km_ref[:, ks])
        band = jnp.where(from_prev, s_prev, s_cur) + bband_ref[k]
        meta = s_meta + bmeta_ref[k]
        m = jnp.max(jnp.maximum(band, meta), axis=-1, keepdims=True)
        p_band = jnp.exp2(band - m)
        p_meta = jnp.exp2(meta - m)
        den = jnp.sum(p_band + p_meta, axis=-1, keepdims=True)
        pb = p_band.astype(BF16)
        zero = jnp.zeros_like(pb)
        o = (_dot(jnp.where(from_prev, pb, zero), vp_ref[:, ks])
             + _dot(jnp.where(from_prev, zero, pb), vc_ref[:, ks])
             + _dot(p_meta.astype(BF16), vm_ref[:, ks]))
        o = o / den
        for half in range(ATTN_GROUP // 2):
            oa = o[(2 * half) * t:(2 * half + 1) * t]
            ob = o[(2 * half + 1) * t:(2 * half + 2) * t]
            o_ref[:, (2 * k + half) * LANES:(2 * k + half + 1) * LANES] = jnp.where(low, oa, ob).astype(BF16)


def _attn_bias_tables(sinks):
    t = CHUNK
    slopes = np.exp2(-8.0 * np.arange(1, ATTN_HEADS + 1, dtype=np.float32) / ATTN_HEADS)
    i = np.arange(t)[:, None]
    j = np.arange(t)[None, :]
    prev = j > i
    dist = np.where(prev, t + i - j, i - j).astype(np.float32)
    band = np.full((3, ATTN_HEADS, t, t), NEG, np.float32)
    alibi = (-slopes[:, None, None] * dist[None]) * np.float32(LOG2E)
    band[1] = np.where(prev[None], NEG, alibi)
    band[2] = alibi
    meta = np.full((3, t, t), NEG, np.float32)
    is_meta_key = j >= t - N_META_TOK
    meta[0] = np.where(np.logical_and(is_meta_key, j <= i), 0.0, NEG)
    meta[1] = np.where(is_meta_key, 0.0, NEG)
    meta[2] = meta[1]
    meta = np.broadcast_to(meta[:, None], (3, ATTN_HEADS, t, t))
    sink_col = (np.arange(t) == 0)[None, None, None, :]
    meta = jnp.where(sink_col, (sinks.astype(F32) * LOG2E)[None, :, None, None], jnp.asarray(meta))
    shape = (3, ATTN_KV_HEADS, ATTN_GROUP * t, t)
    return jnp.asarray(band).reshape(shape), meta.reshape(shape)


N_PROJ_IN = 9
N_SSD_CONST = 8


def _front_kernel(*refs, n_tiles, chunks_per_batch):
    x_ref, meta_ref, nw_ref, wxbc_ref, wdt_ref, wdtt_ref, wq_ref, wk_ref, wv_ref = refs[:N_PROJ_IN]
    ssd_consts = refs[N_PROJ_IN:N_PROJ_IN + N_SSD_CONST]
    bband_ref, bmeta_ref, y_ref, ya_ref = refs[N_PROJ_IN + N_SSD_CONST:N_PROJ_IN + N_SSD_CONST + 4]
    (xbc_s, dt_s, dtt_s, q_s, k_s, v_s, k_carry, v_carry, k_meta, v_meta,
     xbuf, hstate, meta_tail, meta_state) = refs[N_PROJ_IN + N_SSD_CONST + 4:]
    i = pl.program_id(0)
    slot = i % 2

    def project():
        xn = _rms(_tile_tokens(x_ref, meta_ref), nw_ref[...]).astype(BF16)
        xbc_s[slot] = _dot(xn, wxbc_ref[...]).astype(BF16)
        dt_s[slot] = _dot(xn, wdt_ref[...])
        dtt_s[slot] = _dot_nt(wdtt_ref[...], xn)
        q_s[slot] = (_dot(xn, wq_ref[...]) * (ATTN_HEAD_DIM ** -0.5 * LOG2E)).astype(BF16)
        k_s[slot] = _dot(xn, wk_ref[...]).astype(BF16)
        v_s[slot] = _dot(xn, wv_ref[...]).astype(BF16)

    @pl.when(i == 0)
    def _():
        for ref in (k_carry, v_carry, k_meta, v_meta, hstate, meta_tail, meta_state):
            ref[...] = jnp.zeros_like(ref)
        xbuf[:, 0:SUBLANES, :] = jnp.zeros((CONV_TILES, SUBLANES, LANES), F32)
        project()

    @pl.when(i > 0)
    def _():
        project()
        prev_slot = 1 - slot
        tile = i - 1
        for ch in range(CHUNKS_PER_TILE):
            gc = tile * CHUNKS_PER_TILE + ch
            rs = pl.ds(ch * CHUNK, CHUNK)
            _ssd_chunk(gc, ch, xbc_s.at[prev_slot, rs], dt_s.at[prev_slot, rs], dtt_s.at[prev_slot, :, rs],
                       *ssd_consts, y_ref.at[rs], xbuf, hstate, meta_tail, meta_state, chunks_per_batch)
            case = jnp.where(gc <= META_CHUNK, 0,
                             jnp.where((gc - META_CHUNK - 1) % chunks_per_batch == 0, 1, 2))
            if ch == 0:
                kp_ref, vp_ref = k_carry, v_carry
            else:
                ps = pl.ds((ch - 1) * CHUNK, CHUNK)
                kp_ref, vp_ref = k_s.at[prev_slot, ps], v_s.at[prev_slot, ps]
            if ch == META_CHUNK:
                k_meta[...] = jnp.where(tile == 0, k_s[prev_slot, rs], k_meta[...])
                v_meta[...] = jnp.where(tile == 0, v_s[prev_slot, rs], v_meta[...])
            _attn_chunk(q_s.at[prev_slot, rs], k_s.at[prev_slot, rs], kp_ref, k_meta,
                        v_s.at[prev_slot, rs], vp_ref, v_meta, bband_ref.at[case], bmeta_ref.at[case],
                        ya_ref.at[rs])
            if ch == CHUNKS_PER_TILE - 1:
                k_carry[...] = k_s[prev_slot, rs]
                v_carry[...] = v_s[prev_slot, rs]


def _front(x, meta_tile, nw, wxbc, wdt, wdtt, wq, wk, wv, ssd_consts, sinks, chunks_per_batch):
    n_x_tiles = x.shape[0] // TOKEN_TILE
    n_tiles = n_x_tiles + 1
    rows = n_tiles * TOKEN_TILE
    bband, bmeta = _attn_bias_tables(sinks)
    consts = (meta_tile, nw, wxbc, wdt, wdtt, wq, wk, wv) + tuple(ssd_consts) + (bband, bmeta)
    out_spec = lambda n: pl.BlockSpec((TOKEN_TILE, n), lambda i: (jnp.maximum(i - 1, 0), 0))
    tm = TOKEN_TILE
    return pl.pallas_call(
        functools.partial(_front_kernel, n_tiles=n_tiles, chunks_per_batch=chunks_per_batch),
        grid=(n_tiles + 1,),
        in_specs=[_x_spec(n_x_tiles)] + [_const_spec(a.shape) for a in consts],
        out_specs=[out_spec(D_INNER), out_spec(ATTN_WIDTH)],
        out_shape=[jax.ShapeDtypeStruct((rows, D_INNER), BF16), jax.ShapeDtypeStruct((rows, ATTN_WIDTH), BF16)],
        scratch_shapes=[
            pltpu.VMEM((2, tm, CONV_DIM), BF16),
            pltpu.VMEM((2, tm, SSM_HEADS), F32),
            pltpu.VMEM((2, SSM_HEADS, tm), F32),
            pltpu.VMEM((2, tm, ATTN_WIDTH), BF16),
            pltpu.VMEM((2, tm, KV_DUP_WIDTH), BF16),
            pltpu.VMEM((2, tm, KV_DUP_WIDTH), BF16),
            pltpu.VMEM((CHUNK, KV_DUP_WIDTH), BF16),
            pltpu.VMEM((CHUNK, KV_DUP_WIDTH), BF16),
            pltpu.VMEM((CHUNK, KV_DUP_WIDTH), BF16),
            pltpu.VMEM((CHUNK, KV_DUP_WIDTH), BF16),
            pltpu.VMEM((CONV_TILES, CHUNK + SUBLANES, LANES), F32),
            pltpu.VMEM((D_STATE, D_INNER), F32),
            pltpu.VMEM((CONV_TILES, SUBLANES, LANES), F32),
            pltpu.VMEM((D_STATE, D_INNER), F32),
        ],
        compiler_params=_params("front"),
        name="front",
    )(x, *consts)


def _merge_kernel(x_ref, meta_ref, y_ref, z_ref, ya_ref, g_ref, nssm_ref, wso_ref, wao_ref, wmix_ref,
                  npost_ref, npre_ref, h1_ref, hn_ref):
    is_meta = pl.program_id(0) == 0
    blocks = [slice(sb * MERGE_SUB, (sb + 1) * MERGE_SUB) for sb in range(TOKEN_TILE // MERGE_SUB)]

    def gated(rs):
        z = z_ref[rs, :].astype(F32)
        u = y_ref[rs, :].astype(F32) * (z * _sigmoid(z))
        return _rms(u, nssm_ref[...]).astype(BF16)

    def branches(rs, un):
        y_ssm = _dot(un, wso_ref[...])
        y_attn = _dot(ya_ref[rs, :], wao_ref[...])
        gates = _sigmoid(g_ref[rs, :].astype(F32))
        return (gates[:, :D_MODEL] * y_ssm + gates[:, D_MODEL:] * y_attn).astype(BF16)

    def finish(rs, mixed):
        mix = _dot(mixed, wmix_ref[...])
        h1 = jnp.where(is_meta, meta_ref[rs, :], x_ref[rs, :]) + _rms(mix, npost_ref[...])
        h1_ref[rs, :] = h1
        hn_ref[rs, :] = _rms(h1, npre_ref[...]).astype(BF16)

    n = len(blocks)
    un, mixed = {}, {}
    for step in range(n + 2):
        if step - 2 >= 0:
            finish(blocks[step - 2], mixed.pop(step - 2))
        if 0 <= step - 1 < n:
            mixed[step - 1] = branches(blocks[step - 1], un.pop(step - 1))
        if step < n:
            un[step] = gated(blocks[step])


def _merge(x, meta_tile, y, z, ya, g, nssm, wso, wao, wmix, npost, npre):
    rows = y.shape[0]
    tm = TOKEN_TILE
    row_spec = lambda n: pl.BlockSpec((tm, n), lambda i: (i, 0))
    consts = (nssm, wso, wao, wmix, npost, npre)
    return pl.pallas_call(
        _merge_kernel,
        grid=(rows // tm,),
        in_specs=[_x_spec(rows // tm - 1), _const_spec(meta_tile.shape), row_spec(D_INNER), row_spec(D_INNER),
                  row_spec(ATTN_WIDTH), row_spec(2 * D_MODEL)] + [_const_spec(a.shape) for a in consts],
        out_specs=[row_spec(D_MODEL), row_spec(D_MODEL)],
        out_shape=[jax.ShapeDtypeStruct((rows, D_MODEL), F32), jax.ShapeDtypeStruct((rows, D_MODEL), BF16)],
        compiler_params=_params("merge"),
        name="merge",
    )(x, meta_tile, y, z, ya, g, *consts)


def _ffn_kernel(hn_ref, h1_ref, wup_ref, cw_ref, cb_ref, wdn_ref, nw_ref, o_ref, ubuf, tail, meta_tail,
                *, tiles_per_batch):
    i = pl.program_id(0)
    tm = TOKEN_TILE
    is_meta = i == 0
    first_of_batch = (i - 1) % tiles_per_batch == 0

    @pl.when(is_meta)
    def _():
        tail[...] = jnp.zeros_like(tail)
        meta_tail[...] = jnp.zeros_like(meta_tail)

    hn = hn_ref[...]
    acc = jnp.zeros((tm, D_MODEL), F32)
    def up(jc):
        cols = [slice(half * FFN_DIM + jc * FFN_CHUNK, half * FFN_DIM + (jc + 1) * FFN_CHUNK) for half in (0, 1)]
        return [_dot(hn, wup_ref[:, cs]) for cs in cols]

    def down(jc, act):
        return _dot(act, wdn_ref[jc * FFN_CHUNK:(jc + 1) * FFN_CHUNK, :])

    u_next = up(0)
    act_prev = None
    for jc in range(FFN_NCHUNK):
        u = u_next
        if jc + 1 < FFN_NCHUNK:
            u_next = up(jc + 1)
        if act_prev is not None:
            acc = acc + down(jc - 1, act_prev)
        buf = jc % FFN_BUFS
        v_tiles = []
        for j in range(FFN_TILES):
            half, jt = divmod(j, FFN_TILES // 2)
            uj = u[half][:, jt * LANES:(jt + 1) * LANES]
            col = half * FFN_DIM + jc * FFN_CHUNK + jt * LANES
            ls = slice(col, col + LANES)
            ubuf[buf, j, SUBLANES:SUBLANES + tm, :] = uj
            ubuf[buf, j, 0:SUBLANES, :] = jnp.where(first_of_batch, meta_tail[jc, j], tail[jc, j])
            v = cb_ref[:, ls] + cw_ref[FFN_CONV - 1:FFN_CONV, ls] * uj
            for k in range(FFN_CONV - 1):
                off = SUBLANES - (FFN_CONV - 1) + k
                v = v + cw_ref[k:k + 1, ls] * ubuf[buf, j, off:off + tm, :]
            last = uj[tm - SUBLANES:tm, :]
            tail[jc, j] = last
            meta_tail[jc, j] = jnp.where(is_meta, last, meta_tail[jc, j])
            v_tiles.append(v)
        a = jnp.concatenate(v_tiles[:FFN_TILES // 2], axis=1)
        gate = jnp.concatenate(v_tiles[FFN_TILES // 2:], axis=1)
        act_prev = (a * _sigmoid(a) * gate).astype(BF16)
    acc = acc + down(FFN_NCHUNK - 1, act_prev)
    o_ref[...] = h1_ref[...] + _rms(acc, nw_ref[...])


def _ffn(hn, h1, wup, cw, cb, wdn, nw, tiles_per_batch):
    rows = hn.shape[0]
    tm = TOKEN_TILE
    row_spec = lambda n: pl.BlockSpec((tm, n), lambda i: (i, 0))
    consts = (wup, cw, cb, wdn, nw)
    halo = (FFN_NCHUNK, FFN_TILES, SUBLANES, LANES)
    return pl.pallas_call(
        functools.partial(_ffn_kernel, tiles_per_batch=tiles_per_batch),
        grid=(rows // tm,),
        in_specs=[row_spec(D_MODEL), row_spec(D_MODEL)] + [_const_spec(a.shape) for a in consts],
        out_specs=pl.BlockSpec((tm, D_MODEL), lambda i: (jnp.maximum(i - 1, 0), 0)),
        out_shape=jax.ShapeDtypeStruct((rows - tm, D_MODEL), F32),
        scratch_shapes=[
            pltpu.VMEM((FFN_BUFS, FFN_TILES, tm + SUBLANES, LANES), F32),
            pltpu.VMEM(halo, F32),
            pltpu.VMEM(halo, F32),
        ],
        compiler_params=_params("ffn"),
        name="ffn",
    )(hn, h1, *consts)


def kernel(x, meta_tokens, norm_pre_mix, w_in, ssm_conv_w, ssm_conv_b, ssm_dt_bias, ssm_a_log, ssm_d_skip,
           ssm_norm, w_ssm_out, attn_sinks, w_attn_out, w_mix_out, norm_post_mix, norm_pre_ffn, w_ffn_up,
           ffn_conv_w, ffn_conv_b, w_ffn_down, norm_post_ffn):
    batch, seq, d_model = x.shape
    assert d_model == D_MODEL and w_in.shape[0] == 1 and seq % TOKEN_TILE == 0
    assert meta_tokens.shape == (N_META_TOK, D_MODEL)
    n_chunks = seq // CHUNK

    meta_tile = jnp.concatenate(
        [jnp.zeros((TOKEN_TILE - N_META_TOK, D_MODEL), x.dtype), meta_tokens.astype(x.dtype)], axis=0)
    x_flat = x.reshape(batch * seq, D_MODEL)

    w = w_in[0]
    cuts = np.cumsum([0, D_INNER, CONV_DIM, SSM_HEADS, ATTN_WIDTH, ATTN_KV_HEADS * ATTN_HEAD_DIM,
                      ATTN_KV_HEADS * ATTN_HEAD_DIM, 2 * D_MODEL])
    wz, wxbc, wdt, wq, wk, wv, wg = [w[:, cuts[n]:cuts[n + 1]] for n in range(7)]

    def dup_heads(a):
        a = a.reshape(D_MODEL, ATTN_KV_HEADS, 1, ATTN_HEAD_DIM)
        return jnp.broadcast_to(a, (D_MODEL, ATTN_KV_HEADS, 2, ATTN_HEAD_DIM)).reshape(D_MODEL, KV_DUP_WIDTH)

    row = lambda a: a.reshape(1, -1).astype(F32)
    nw = row(norm_pre_mix[0])
    z, gates = _zg_proj(x_flat, meta_tile, nw, wz.astype(BF16), wg.astype(BF16))
    ssd_consts = _ssd_consts(ssm_conv_w[0].astype(F32), ssm_conv_b[0].astype(F32), ssm_dt_bias[0].astype(F32),
                             ssm_a_log[0].astype(F32), ssm_d_skip[0].astype(F32))
    y, ya = _front(x_flat, meta_tile, nw, wxbc.astype(BF16), wdt.astype(BF16), wdt.T.astype(BF16),
                   wq.astype(BF16), dup_heads(wk).astype(BF16), dup_heads(wv).astype(BF16), ssd_consts,
                   attn_sinks[0], n_chunks)
    h1, hn = _merge(x_flat, meta_tile, y, z, ya, gates, row(ssm_norm[0]), w_ssm_out[0].astype(BF16),
                    w_attn_out[0].astype(BF16), w_mix_out[0].astype(BF16), row(norm_post_mix[0]),
                    row(norm_pre_ffn[0]))
    out = _ffn(hn, h1, w_ffn_up[0].astype(BF16), ffn_conv_w[0].astype(F32), row(ffn_conv_b[0]),
               w_ffn_down[0].astype(BF16), row(norm_post_ffn[0]), seq // TOKEN_TILE)
    return out.reshape(batch, seq, D_MODEL)
```
